```python
import math
import jax, jax.numpy as jnp
from jax import lax
import numpy as np

D_MODEL = 2048
BATCH = 2
SEQ = 8192
DEPTH = 1

CHUNK = 64
EPS = 1e-6
M_HEADS = 8
M_HEAD_DIM = D_MODEL // M_HEADS
M_WIDTH = M_HEADS * M_HEAD_DIM
M_CONV = 4
A_HEADS = 8
A_QK_DIM = D_MODEL // (2 * A_HEADS * 2) * 2
A_V_DIM = 2 * A_QK_DIM
A_QK_WIDTH = A_HEADS * 2 * A_QK_DIM
A_WIDTH = A_HEADS * A_V_DIM
Q_BLOCK = 128
ALIBI_SLOPES = 2.0 ** (-8.0 * np.arange(1, A_HEADS + 1) / A_HEADS)
D_FF = 5632
FF_CONV = 3
IN_SIZES = (2 * M_WIDTH,
            M_WIDTH,
            M_WIDTH,
            2 * M_HEADS,
            A_QK_WIDTH,
            A_QK_WIDTH,
            A_WIDTH,
            2 * D_MODEL)
IN_TOTAL = int(sum(IN_SIZES))
IN_SPLITS = [int(s) for s in np.cumsum(IN_SIZES)[:-1]]

kernel_name = 'hybrid_mlstm_diffattn_convffn'


def rmsnorm(x, g):
    xf = x.astype(jnp.float32)
    y = xf * lax.rsqrt(jnp.mean(xf * xf, axis=-1, keepdims=True) + EPS)
    return (y * g.astype(jnp.float32)).astype(x.dtype)


def head_rmsnorm(x, g):
    H, Dh = x.shape[2], x.shape[3]
    return rmsnorm(x, g.reshape(H, Dh))


def causal_dwconv(x, w, b):
    K, C = w.shape
    y = lax.conv_general_dilated(x, w[:, None, :].astype(x.dtype), window_strides=(1,),
                                 padding=[(K - 1, 0)], dimension_numbers=('NWC', 'WIO', 'NWC'),
                                 feature_group_count=C)
    return y + b.astype(x.dtype)


def mlstm_chunkwise(q, k, v, i_pre, logf):
    B, S, H, Dk = q.shape
    Dv = v.shape[-1]
    N = S // CHUNK
    f32 = jnp.float32

    def to_chunks(t):
        t = t.astype(f32).reshape((B, N, CHUNK, H) + t.shape[3:])
        return jnp.moveaxis(t, (1, 3), (0, 2))

    tri = jnp.tril(jnp.ones((CHUNK, CHUNK), bool))

    def step(carry, inp):
        C, n, m = carry
        qc, kc, vc, ic, fc = inp
        b = jnp.cumsum(fc, axis=-1)
        a = b + m[..., None]
        dmat = b[..., :, None] - b[..., None, :] + ic[..., None, :]
        dmat = jnp.where(tri, dmat, -jnp.inf)
        m_t = jnp.maximum(a, jnp.max(dmat, axis=-1))
        w_ts = jnp.exp(dmat - m_t[..., None])
        w_inter = jnp.exp(a - m_t)
        s_qk = jnp.einsum('bhtd,bhsd->bhts', qc, kc) * w_ts
        num = (jnp.einsum('bhts,bhsv->bhtv', s_qk, vc)
               + w_inter[..., None] * jnp.einsum('bhtk,bhkv->bhtv', qc, C))
        den = jnp.sum(s_qk, axis=-1) + w_inter * jnp.einsum('bhtk,bhk->bht', qc, n)
        h = num / jnp.maximum(jnp.abs(den), jnp.exp(-m_t))[..., None]
        b_last = b[..., -1]
        logw_s = b_last[..., None] - b + ic
        m_new = jnp.maximum(b_last + m, jnp.max(logw_s, axis=-1))
        ws = jnp.exp(logw_s - m_new[..., None])
        decay = jnp.exp(b_last + m - m_new)
        kw = kc * ws[..., None]
        C_new = decay[..., None, None] * C + jnp.einsum('bhsk,bhsv->bhkv', kw, vc)
        n_new = decay[..., None] * n + jnp.sum(kw, axis=2)
        return (C_new, n_new, m_new), h

    init = (jnp.zeros((B, H, Dk, Dv), f32), jnp.zeros((B, H, Dk), f32), jnp.zeros((B, H), f32))
    xs = (to_chunks(q) * (Dk ** -0.5), to_chunks(k), to_chunks(v), to_chunks(i_pre), to_chunks(logf))
    _, h = lax.scan(step, init, xs)
    return jnp.moveaxis(h, (0, 2), (1, 3)).reshape(B, S, H, Dv)


def diff_attention(q, k, v, lam):
    B, S, H, _, Dk = q.shape
    scale = Dk ** -0.5
    slopes = jnp.asarray(ALIBI_SLOPES, jnp.float32)
    lam = lam.astype(jnp.float32)
    outs = []
    for j in range(S // Q_BLOCK):
        start, end = j * Q_BLOCK, (j + 1) * Q_BLOCK
        qpos = jnp.arange(start, end)
        kpos = jnp.arange(end)
        s = jnp.einsum('bqhcd,bshcd->bhcqs', q[:, start:end], k[:, :end]).astype(jnp.float32) * scale
        dist = jnp.abs(qpos[:, None] - kpos[None, :]).astype(jnp.float32)
        allowed = (kpos[None, :] // CHUNK) <= (qpos[:, None] // CHUNK)
        s = jnp.where(allowed, s - slopes[:, None, None, None] * dist, -1e30)
        p = jax.nn.softmax(s, axis=-1)
        attn = p[:, :, 0] - lam * p[:, :, 1]
        outs.append(jnp.einsum('bhqs,bshd->bqhd', attn.astype(v.dtype), v[:, :end]))
    return jnp.concatenate(outs, axis=1)


def setup_inputs(seed: int = 0) -> dict:
    key = jax.random.key(seed)
    ks = jax.random.split(key, 24)
    f32 = jnp.float32

    def nrm(k, shape, scale):
        return jax.random.normal(k, shape, f32) * scale

    def gain(k, n):
        return 1.0 + nrm(k, (DEPTH, n), 0.02)

    b_i = nrm(ks[3], (DEPTH, M_HEADS), 0.1)
    b_f = jnp.linspace(3.0, 6.0, M_HEADS, dtype=f32)[None, :] + nrm(ks[4], (DEPTH, M_HEADS), 0.1)
    return {
        'x': nrm(ks[0], (BATCH, SEQ, D_MODEL), 1.0),
        'g_pre_mix': gain(ks[1], D_MODEL),
        'w_in': nrm(ks[2], (DEPTH, D_MODEL, IN_TOTAL), D_MODEL ** -0.5),
        'b_gates': jnp.concatenate([b_i, b_f], axis=-1),
        'm_conv_w': nrm(ks[5], (DEPTH, M_CONV, 2 * M_WIDTH), M_CONV ** -0.5),
        'm_conv_b': nrm(ks[6], (DEPTH, 2 * M_WIDTH), 0.02),
        'm_head_norm': gain(ks[7], M_WIDTH),
        'lambda_q1': nrm(ks[8], (DEPTH, A_QK_DIM), 0.1),
        'lambda_k1': nrm(ks[9], (DEPTH, A_QK_DIM), 0.1),
        'lambda_q2': nrm(ks[10], (DEPTH, A_QK_DIM), 0.1),
        'lambda_k2': nrm(ks[11], (DEPTH, A_QK_DIM), 0.1),
        'a_head_norm': gain(ks[12], A_WIDTH),
        'w_branch_m': nrm(ks[13], (DEPTH, M_WIDTH, D_MODEL), M_WIDTH ** -0.5),
        'w_branch_a': nrm(ks[14], (DEPTH, A_WIDTH, D_MODEL), A_WIDTH ** -0.5),
        'w_out': nrm(ks[15], (DEPTH, D_MODEL, D_MODEL), D_MODEL ** -0.5),
        'g_post_mix': gain(ks[16], D_MODEL),
        'g_pre_ffn': gain(ks[17], D_MODEL),
        'w_up': nrm(ks[18], (DEPTH, D_MODEL, 2 * D_FF), D_MODEL ** -0.5),
        'ffn_conv_w': nrm(ks[19], (DEPTH, FF_CONV, 2 * D_FF), FF_CONV ** -0.5),
        'ffn_conv_b': nrm(ks[20], (DEPTH, 2 * D_FF), 0.02),
        'w_down': nrm(ks[21], (DEPTH, D_FF, D_MODEL), D_FF ** -0.5),
        'g_post_ffn': gain(ks[22], D_MODEL),
    }


def reference(x, g_pre_mix, w_in, b_gates, m_conv_w, m_conv_b, m_head_norm,
              lambda_q1, lambda_k1, lambda_q2, lambda_k2, a_head_norm,
              w_branch_m, w_branch_a, w_out, g_post_mix, g_pre_ffn,
              w_up, ffn_conv_w, ffn_conv_b, w_down, g_post_ffn):
    B, S, _ = x.shape
    for l in range(DEPTH):
        lam_init = 0.8 - 0.6 * math.exp(-0.3 * l)
        h = rmsnorm(x, g_pre_mix[l])
        proj = h @ w_in[l]
        m_qk, m_v, m_o, m_if, a_q, a_k, a_v, merge = jnp.split(proj, IN_SPLITS, axis=-1)
        m_qk = jax.nn.silu(causal_dwconv(m_qk, m_conv_w[l], m_conv_b[l]))
        m_q, m_k = jnp.split(m_qk, 2, axis=-1)
        m_if = m_if.astype(jnp.float32) + b_gates[l].astype(jnp.float32)
        i_pre, f_pre = jnp.split(m_if, 2, axis=-1)
        hm = mlstm_chunkwise(m_q.reshape(B, S, M_HEADS, M_HEAD_DIM),
                             m_k.reshape(B, S, M_HEADS, M_HEAD_DIM),
                             m_v.reshape(B, S, M_HEADS, M_HEAD_DIM),
                             i_pre, jax.nn.log_sigmoid(f_pre))
        hm = head_rmsnorm(hm, m_head_norm[l]).reshape(B, S, M_WIDTH).astype(x.dtype)
        y_m = (jax.nn.sigmoid(m_o) * hm) @ w_branch_m[l]
        lam = (jnp.exp(jnp.dot(lambda_q1[l].astype(jnp.float32), lambda_k1[l].astype(jnp.float32)))
               - jnp.exp(jnp.dot(lambda_q2[l].astype(jnp.float32), lambda_k2[l].astype(jnp.float32)))
               + lam_init)
        ha = diff_attention(a_q.reshape(B, S, A_HEADS, 2, A_QK_DIM),
                            a_k.reshape(B, S, A_HEADS, 2, A_QK_DIM),
                            a_v.reshape(B, S, A_HEADS, A_V_DIM), lam)
        ha = head_rmsnorm(ha, a_head_norm[l]) * (1.0 - lam_init)
        y_a = ha.reshape(B, S, A_WIDTH) @ w_branch_a[l]
        g_m, g_a = jnp.split(jax.nn.sigmoid(merge), 2, axis=-1)
        mix = (g_m * y_m + g_a * y_a) @ w_out[l]
        x = x + rmsnorm(mix, g_post_mix[l])
        h = rmsnorm(x, g_pre_ffn[l])
        u = causal_dwconv(h @ w_up[l], ffn_conv_w[l], ffn_conv_b[l])
        gate, val = jnp.split(u, 2, axis=-1)
        y = (jax.nn.gelu(gate, approximate=True) * val) @ w_down[l]
        x = x + rmsnorm(y, g_post_ffn[l])
    return x
```

```python
import functools
import math

import numpy as np
import jax
import jax.numpy as jnp
from jax import lax
from jax.experimental import pallas as pl
from jax.experimental.pallas import tpu as pltpu

F32 = jnp.float32
BF16 = jnp.bfloat16

EPS = 1e-6
CHUNK = 64
M_HEADS = 8
M_HEAD_DIM = 256
M_CONV = 4
A_HEADS = 8
A_QK_DIM = 128
A_V_DIM = 256
FF_CONV = 3
ALIBI_SLOPES = 2.0 ** (-8.0 * np.arange(1, A_HEADS + 1) / A_HEADS)

V7X_LANES = 128
V7X_SUBLANES = 8
V7X_SCOPED_VMEM_BYTES = 60000 * 1024
COMPILER_TEMP_BYTES = 16 * 1024 * 1024

MLSTM_CHUNK = 256
ATTN_TQ = 512
ATTN_TK = 512
MASK_VALUE = -1e30


def _vmem_limit(*buffer_bytes):
    return int(min(sum(buffer_bytes) + COMPILER_TEMP_BYTES, V7X_SCOPED_VMEM_BYTES))


def _nbytes(shape, dtype):
    return int(np.prod(shape)) * jnp.dtype(dtype).itemsize


def _params(semantics, *buffer_bytes):
    return pltpu.CompilerParams(dimension_semantics=semantics,
                                vmem_limit_bytes=_vmem_limit(*buffer_bytes))


def _rms_scale(y):
    return lax.rsqrt(jnp.mean(y * y, axis=-1, keepdims=True) + EPS)


def _rmsnorm_kernel(x_ref, g_ref, o_ref):
    x = x_ref[...]
    o_ref[...] = (x * _rms_scale(x) * g_ref[...]).astype(o_ref.dtype)


def _rmsnorm(x, g, tm=512):
    T, D = x.shape
    return pl.pallas_call(
        _rmsnorm_kernel,
        grid=(T // tm,),
        in_specs=[pl.BlockSpec((tm, D), lambda i: (i, 0)),
                  pl.BlockSpec((1, D), lambda i: (0, 0))],
        out_specs=pl.BlockSpec((tm, D), lambda i: (i, 0)),
        out_shape=jax.ShapeDtypeStruct((T, D), BF16),
        compiler_params=_params(("parallel",), 2 * _nbytes((tm, D), F32), 2 * _nbytes((tm, D), BF16)),
        name="rmsnorm",
    )(x, g.reshape(1, D))


def _causal_conv_rows(acc, buf_ref, carry_ref, first_tile, cw_ref, cb_ref, taps):
    tm = acc.shape[0]
    halo = V7X_SUBLANES

    @pl.when(first_tile)
    def _():
        buf_ref[0:halo, :] = jnp.zeros((halo, acc.shape[1]), F32)

    @pl.when(jnp.logical_not(first_tile))
    def _():
        buf_ref[0:halo, :] = carry_ref[...]

    buf_ref[halo:halo + tm, :] = acc
    carry_ref[...] = acc[tm - halo:tm, :]
    y = cb_ref[...] + cw_ref[taps - 1:taps, :] * acc
    for k in range(taps - 1):
        off = halo - (taps - 1) + k
        y = y + cw_ref[k:k + 1, :] * buf_ref[off:off + tm, :]
    return y


def _proj_conv_silu_kernel(h_ref, w_ref, cw_ref, cb_ref, sc_ref, o_ref, carry_ref, buf_ref, *,
                           tiles_per_seq, taps):
    i = pl.program_id(0)
    j = pl.program_id(1)
    acc = jnp.dot(h_ref[...], w_ref[...], preferred_element_type=F32)
    y = _causal_conv_rows(acc, buf_ref, carry_ref.at[j], (i % tiles_per_seq) == 0, cw_ref, cb_ref, taps)
    y = y * jax.nn.sigmoid(y)
    o_ref[...] = (y * sc_ref[...]).astype(o_ref.dtype)


def _proj_conv_silu(h, w, cw, cb, scale_row, seq_len, tm=1024, tn=1024):
    T, K = h.shape
    N = w.shape[1]
    taps = cw.shape[0]
    nj = N // tn
    kern = functools.partial(_proj_conv_silu_kernel, tiles_per_seq=seq_len // tm, taps=taps)
    return pl.pallas_call(
        kern,
        grid=(T // tm, nj),
        in_specs=[pl.BlockSpec((tm, K), lambda i, j: (i, 0)),
                  pl.BlockSpec((K, tn), lambda i, j: (0, j)),
                  pl.BlockSpec((taps, tn), lambda i, j: (0, j)),
                  pl.BlockSpec((1, tn), lambda i, j: (0, j)),
                  pl.BlockSpec((1, tn), lambda i, j: (0, j))],
        out_specs=pl.BlockSpec((tm, tn), lambda i, j: (i, j)),
        out_shape=jax.ShapeDtypeStruct((T, N), BF16),
        scratch_shapes=[pltpu.VMEM((nj, V7X_SUBLANES, tn), F32),
                        pltpu.VMEM((tm + V7X_SUBLANES, tn), F32)],
        compiler_params=_params(("arbitrary", "arbitrary"),
                                2 * _nbytes((tm, K), BF16), 2 * _nbytes((K, tn), BF16),
                                2 * _nbytes((tm, tn), BF16), 2 * _nbytes((tm, tn), F32)),
        name="proj_conv_silu",
    )(h, w, cw, cb.reshape(1, N), scale_row.reshape(1, N))


def _proj_act_kernel(h_ref, w_ref, b_ref, o_ref, *, act):
    acc = jnp.dot(h_ref[...], w_ref[...], preferred_element_type=F32) + b_ref[...]
    if act == "sigmoid":
        acc = jax.nn.sigmoid(acc)
    o_ref[...] = acc.astype(o_ref.dtype)


def _proj_act(h, w, bias, act, out_dtype, tm=1024, tn=1024):
    T, K = h.shape
    N = w.shape[1]
    tn = min(tn, N)
    return pl.pallas_call(
        functools.partial(_proj_act_kernel, act=act),
        grid=(T // tm, N // tn),
        in_specs=[pl.BlockSpec((tm, K), lambda i, j: (i, 0)),
                  pl.BlockSpec((K, tn), lambda i, j: (0, j)),
                  pl.BlockSpec((1, tn), lambda i, j: (0, j))],
        out_specs=pl.BlockSpec((tm, tn), lambda i, j: (i, j)),
        out_shape=jax.ShapeDtypeStruct((T, N), out_dtype),
        compiler_params=_params(("parallel", "parallel"),
                                2 * _nbytes((tm, K), BF16), 2 * _nbytes((K, tn), BF16),
                                2 * _nbytes((tm, tn), out_dtype), _nbytes((tm, tn), F32)),
        name="proj_" + act,
    )(h, w, bias.reshape(1, N))


def _mlstm_kernel(q_ref, k_ref, v_ref, og_ref, g_ref, gain_ref, o_ref, c_ref, n_ref, m_ref, *,
                  heads, head_dim):
    L = q_ref.shape[0]
    H, Dh = heads, head_dim

    @pl.when(pl.program_id(1) == 0)
    def _():
        c_ref[...] = jnp.zeros_like(c_ref)
        n_ref[...] = jnp.zeros_like(n_ref)
        m_ref[...] = jnp.zeros_like(m_ref)

    gates = g_ref[...]
    logf = jnp.minimum(gates, 0.0) - jnp.log1p(jnp.exp(-jnp.abs(gates)))
    row = lax.broadcasted_iota(jnp.int32, (L, L), 0)
    col = lax.broadcasted_iota(jnp.int32, (L, L), 1)
    tri = col <= row
    cum = jnp.dot(tri.astype(F32), logf, precision=lax.Precision.HIGHEST,
                  preferred_element_type=F32)
    gates_t = gates.T
    cum_t = cum.T

    for h in range(H):
        hs = slice(h * Dh, (h + 1) * Dh)
        b_col = cum[:, H + h:H + h + 1]
        u_col = gates[:, h:h + 1] - b_col
        u_row = gates_t[h:h + 1, :] - cum_t[H + h:H + h + 1, :]
        m_prev = m_ref[h, 0:1, 0:1]
        cu = jnp.max(jnp.where(tri, u_row, -jnp.inf), axis=1, keepdims=True)
        mm = jnp.maximum(m_prev, cu)
        w_intra = jnp.where(tri, jnp.exp(u_row - mm), 0.0)
        w_inter = jnp.exp(m_prev - mm)

        qh = q_ref[:, hs]
        kh = k_ref[:, hs]
        vh = v_ref[:, hs]
        s = lax.dot_general(qh, kh, (((1,), (1,)), ((), ())), preferred_element_type=F32)
        sqk = s * w_intra
        num = (jnp.dot(sqk.astype(BF16), vh, preferred_element_type=F32)
               + w_inter * jnp.dot(qh, c_ref[h].astype(BF16), preferred_element_type=F32))
        qn = jnp.sum(qh.astype(F32) * n_ref[h], axis=1, keepdims=True)
        den = jnp.sum(sqk, axis=1, keepdims=True) + w_inter * qn
        hh = num / jnp.maximum(jnp.abs(den), jnp.exp(-(b_col + mm)))
        y = hh * _rms_scale(hh) * gain_ref[:, hs]
        o_ref[:, hs] = (y * og_ref[:, hs].astype(F32)).astype(o_ref.dtype)

        mm_last = mm[L - 1:L, :]
        ws = jnp.exp(u_col - mm_last)
        decay = jnp.exp(m_prev - mm_last)
        kw = kh.astype(F32) * ws
        c_ref[h] = decay * c_ref[h] + lax.dot_general(
            kw.astype(BF16), vh, (((0,), (0,)), ((), ())), preferred_element_type=F32)
        n_ref[h] = decay * n_ref[h] + jnp.sum(kw, axis=0, keepdims=True)
        m_ref[h] = jnp.broadcast_to(b_col[L - 1:L, :] + mm_last, m_ref.shape[1:])


def _mlstm(qk, plain, sig, gates, gain, batch, seq_len):
    T = qk.shape[0]
    H, Dh, L = M_HEADS, M_HEAD_DIM, MLSTM_CHUNK
    W = H * Dh
    nt = seq_len // L
    row_blk = lambda b, t: (b * nt + t, 0)
    return pl.pallas_call(
        functools.partial(_mlstm_kernel, heads=H, head_dim=Dh),
        grid=(batch, nt),
        in_specs=[pl.BlockSpec((L, W), row_blk),
                  pl.BlockSpec((L, W), lambda b, t: (b * nt + t, 1)),
                  pl.BlockSpec((L, W), row_blk),
                  pl.BlockSpec((L, W), row_blk),
                  pl.BlockSpec((L, V7X_LANES), row_blk),
                  pl.BlockSpec((1, W), lambda b, t: (0, 0))],
        out_specs=pl.BlockSpec((L, W), row_blk),
        out_shape=jax.ShapeDtypeStruct((T, W), BF16),
        scratch_shapes=[pltpu.VMEM((H, Dh, Dh), F32),
                        pltpu.VMEM((H, 1, Dh), F32),
                        pltpu.VMEM((H, V7X_SUBLANES, V7X_LANES), F32)],
        compiler_params=_params(("arbitrary", "arbitrary"),
                                10 * _nbytes((L, W), BF16), _nbytes((H, Dh, Dh), F32)),
        name="mlstm",
    )(qk, qk, plain, sig, gates, gain.reshape(1, W))


def _attn_kernel(slopes_ref, lam_ref, q_ref, k_ref, v_ref, gain_ref, o_ref,
                 acc_ref, m_ref, l_ref, *, tq, tk, lam_init):
    h = pl.program_id(1)
    qi = pl.program_id(2)
    dk = A_QK_DIM
    scale = dk ** -0.5
    slope = slopes_ref[h]

    m_ref[...] = jnp.full(m_ref.shape, MASK_VALUE, F32)
    l_ref[...] = jnp.zeros_like(l_ref)
    acc_ref[...] = jnp.zeros_like(acc_ref)

    row = lax.broadcasted_iota(jnp.int32, (tq, tk), 0)
    col = lax.broadcasted_iota(jnp.int32, (tq, tk), 1)
    rel = (row - col).astype(F32)

    def update(j, bias):
        kt = k_ref[pl.ds(pl.multiple_of(j * tk, tk), tk), :]
        vt = v_ref[pl.ds(pl.multiple_of(j * tk, tk), tk), :]
        for c in range(2):
            cs = slice(c * dk, (c + 1) * dk)
            s = lax.dot_general(q_ref[:, cs], kt[:, cs], (((1,), (1,)), ((), ())),
                                preferred_element_type=F32) * scale + bias
            m_prev = m_ref[c]
            m_new = jnp.maximum(m_prev, jnp.max(s, axis=1, keepdims=True))
            alpha = jnp.exp(m_prev - m_new)
            p = jnp.exp(s - m_new)
            l_ref[c] = alpha * l_ref[c] + jnp.sum(p, axis=1, keepdims=True)
            acc_ref[c] = alpha * acc_ref[c] + jnp.dot(p.astype(BF16), vt, preferred_element_type=F32)
            m_ref[c] = m_new

    def full_tile(j, carry):
        base = ((qi - j) * tk).astype(F32)
        update(j, -slope * (rel + base))
        return carry

    lax.fori_loop(0, qi, full_tile, 0)

    allowed = (col // CHUNK) <= (row // CHUNK)
    update(qi, jnp.where(allowed, -slope * jnp.abs(rel), MASK_VALUE))

    lam_vec = lam_ref[...]
    lam = (jnp.exp(jnp.sum(lam_vec[0:1] * lam_vec[1:2], axis=1, keepdims=True))
           - jnp.exp(jnp.sum(lam_vec[2:3] * lam_vec[3:4], axis=1, keepdims=True)) + lam_init)
    o = acc_ref[0] / l_ref[0] - lam * (acc_ref[1] / l_ref[1])
    o_ref[...] = (o * _rms_scale(o) * gain_ref[...] * (1.0 - lam_init)).astype(o_ref.dtype)


def _diff_attention(plain, lam_vecs, gain, lam_init, batch, seq_len):
    T = plain.shape[0]
    H, Dv = A_HEADS, A_V_DIM
    W = H * Dv
    tq, tk = ATTN_TQ, ATTN_TK
    assert tq == tk and tq % CHUNK == 0 and seq_len % tq == 0
    nq = seq_len // tq
    col0 = W // Dv
    slopes = jnp.asarray(ALIBI_SLOPES, F32)
    return pl.pallas_call(
        functools.partial(_attn_kernel, tq=tq, tk=tk, lam_init=lam_init),
        grid=(batch, H, nq),
        in_specs=[pl.BlockSpec(memory_space=pltpu.SMEM),
                  pl.BlockSpec((4, A_QK_DIM), lambda b, h, i: (0, 0)),
                  pl.BlockSpec((tq, Dv), lambda b, h, i: (b * nq + i, col0 + h)),
                  pl.BlockSpec((seq_len, Dv), lambda b, h, i: (b, 2 * col0 + h)),
                  pl.BlockSpec((seq_len, Dv), lambda b, h, i: (b, 3 * col0 + h)),
                  pl.BlockSpec((1, Dv), lambda b, h, i: (0, h))],
        out_specs=pl.BlockSpec((tq, Dv), lambda b, h, i: (b * nq + i, h)),
        out_shape=jax.ShapeDtypeStruct((T, W), BF16),
        scratch_shapes=[pltpu.VMEM((2, tq, Dv), F32),
                        pltpu.VMEM((2, tq, 1), F32),
                        pltpu.VMEM((2, tq, 1), F32)],
        compiler_params=_params(("parallel", "parallel", "arbitrary"),
                                4 * _nbytes((seq_len, Dv), BF16), 4 * _nbytes((tq, Dv), BF16),
                                2 * _nbytes((tq, Dv), F32), 4 * _nbytes((tq, V7X_LANES), F32),
                                6 * _nbytes((tq, tk), F32)),
        name="diff_attention",
    )(slopes, lam_vecs, plain, plain, plain, gain.reshape(1, W))


def _merge_kernel(am_ref, aa_ref, wm_ref, wa_ref, gm_ref, ga_ref, o_ref):
    ym = jnp.dot(am_ref[...], wm_ref[...], preferred_element_type=F32)
    ya = jnp.dot(aa_ref[...], wa_ref[...], preferred_element_type=F32)
    o_ref[...] = (gm_ref[...].astype(F32) * ym + ga_ref[...].astype(F32) * ya).astype(o_ref.dtype)


def _branch_merge(hm, ha, wm, wa, sig, tm=1024, tn=1024):
    T, K = hm.shape
    N = wm.shape[1]
    nj = N // tn
    return pl.pallas_call(
        _merge_kernel,
        grid=(T // tm, nj),
        in_specs=[pl.BlockSpec((tm, K), lambda i, j: (i, 0)),
                  pl.BlockSpec((tm, K), lambda i, j: (i, 0)),
                  pl.BlockSpec((K, tn), lambda i, j: (0, j)),
                  pl.BlockSpec((K, tn), lambda i, j: (0, j)),
                  pl.BlockSpec((tm, tn), lambda i, j: (i, nj + j)),
                  pl.BlockSpec((tm, tn), lambda i, j: (i, 2 * nj + j))],
        out_specs=pl.BlockSpec((tm, tn), lambda i, j: (i, j)),
        out_shape=jax.ShapeDtypeStruct((T, N), BF16),
        compiler_params=_params(("parallel", "parallel"),
                                4 * _nbytes((tm, K), BF16), 4 * _nbytes((K, tn), BF16),
                                6 * _nbytes((tm, tn), BF16), 2 * _nbytes((tm, tn), F32)),
        name="branch_merge",
    )(hm, ha, wm, wa, sig, sig)


def _outproj_kernel(mix_ref, w_ref, x_ref, gpost_ref, gpre_ref, x1_ref, h2_ref):
    y = jnp.dot(mix_ref[...], w_ref[...], preferred_element_type=F32)
    x1 = x_ref[...] + y * _rms_scale(y) * gpost_ref[...]
    x1_ref[...] = x1
    h2_ref[...] = (x1 * _rms_scale(x1) * gpre_ref[...]).astype(h2_ref.dtype)


def _outproj(mix, w, x, g_post, g_pre, tm=512):
    T, K = mix.shape
    D = w.shape[1]
    row = lambda i: (i, 0)
    fixed = lambda i: (0, 0)
    return pl.pallas_call(
        _outproj_kernel,
        grid=(T // tm,),
        in_specs=[pl.BlockSpec((tm, K), row),
                  pl.BlockSpec((K, D), fixed),
                  pl.BlockSpec((tm, D), row),
                  pl.BlockSpec((1, D), fixed),
                  pl.BlockSpec((1, D), fixed)],
        out_specs=[pl.BlockSpec((tm, D), row), pl.BlockSpec((tm, D), row)],
        out_shape=[jax.ShapeDtypeStruct((T, D), F32), jax.ShapeDtypeStruct((T, D), BF16)],
        compiler_params=_params(("parallel",),
                                2 * _nbytes((tm, K), BF16), 2 * _nbytes((K, D), BF16),
                                4 * _nbytes((tm, D), F32), 2 * _nbytes((tm, D), BF16),
                                2 * _nbytes((tm, D), F32)),
        name="outproj_norm",
    )(mix, w, x, g_post.reshape(1, D), g_pre.reshape(1, D))


def _gelu_tanh(x):
    return 0.5 * x * (1.0 + jnp.tanh(math.sqrt(2.0 / math.pi) * (x + 0.044715 * (x * x * x))))


def _ffn_up_kernel(h_ref, wg_ref, wv_ref, cwg_ref, cwv_ref, cbg_ref, cbv_ref, o_ref,
                   carry_ref, buf_ref, *, tiles_per_seq, taps):
    i = pl.program_id(0)
    j = pl.program_id(1)
    first = (i % tiles_per_seq) == 0
    h = h_ref[...]
    gate = _causal_conv_rows(jnp.dot(h, wg_ref[...], preferred_element_type=F32),
                             buf_ref.at[0], carry_ref.at[j, 0], first, cwg_ref, cbg_ref, taps)
    val = _causal_conv_rows(jnp.dot(h, wv_ref[...], preferred_element_type=F32),
                            buf_ref.at[1], carry_ref.at[j, 1], first, cwv_ref, cbv_ref, taps)
    o_ref[...] = (_gelu_tanh(gate) * val).astype(o_ref.dtype)


def _ffn_up(h2, w_up, cw, cb, seq_len, tm=1024, tf=512):
    T, K = h2.shape
    F = w_up.shape[1] // 2
    taps = cw.shape[0]
    nj = F // tf
    gate_col = lambda i, j: (0, j)
    val_col = lambda i, j: (0, nj + j)
    cb2 = cb.reshape(1, 2 * F)
    return pl.pallas_call(
        functools.partial(_ffn_up_kernel, tiles_per_seq=seq_len // tm, taps=taps),
        grid=(T // tm, nj),
        in_specs=[pl.BlockSpec((tm, K), lambda i, j: (i, 0)),
                  pl.BlockSpec((K, tf), gate_col),
                  pl.BlockSpec((K, tf), val_col),
                  pl.BlockSpec((taps, tf), gate_col),
                  pl.BlockSpec((taps, tf), val_col),
                  pl.BlockSpec((1, tf), gate_col),
                  pl.BlockSpec((1, tf), val_col)],
        out_specs=pl.BlockSpec((tm, tf), lambda i, j: (i, j)),
        out_shape=jax.ShapeDtypeStruct((T, F), BF16),
        scratch_shapes=[pltpu.VMEM((nj, 2, V7X_SUBLANES, tf), F32),
                        pltpu.VMEM((2, tm + V7X_SUBLANES, tf), F32)],
        compiler_params=_params(("arbitrary", "arbitrary"),
                                2 * _nbytes((tm, K), BF16), 4 * _nbytes((K, tf), BF16),
                                2 * _nbytes((tm, tf), BF16), 4 * _nbytes((tm, tf), F32)),
        name="ffn_up_conv_gelu",
    )(h2, w_up, w_up, cw, cw, cb2, cb2)


def _ffn_down_kernel(a_ref, w_ref, x1_ref, g_ref, o_ref, acc_ref):
    k = pl.program_id(1)

    @pl.when(k == 0)
    def _():
        acc_ref[...] = jnp.zeros_like(acc_ref)

    acc_ref[...] += jnp.dot(a_ref[...], w_ref[...], preferred_element_type=F32)

    @pl.when(k == pl.num_programs(1) - 1)
    def _():
        y = acc_ref[...]
        o_ref[...] = x1_ref[...] + y * _rms_scale(y) * g_ref[...]


def _ffn_down(act, w, x1, g, tm=1024, tk=512):
    T, F = act.shape
    D = w.shape[1]
    return pl.pallas_call(
        _ffn_down_kernel,
        grid=(T // tm, F // tk),
        in_specs=[pl.BlockSpec((tm, tk), lambda i, k: (i, k)),
                  pl.BlockSpec((tk, D), lambda i, k: (k, 0)),
                  pl.BlockSpec((tm, D), lambda i, k: (i, 0)),
                  pl.BlockSpec((1, D), lambda i, k: (0, 0))],
        out_specs=pl.BlockSpec((tm, D), lambda i, k: (i, 0)),
        out_shape=jax.ShapeDtypeStruct((T, D), F32),
        scratch_shapes=[pltpu.VMEM((tm, D), F32)],
        compiler_params=_params(("parallel", "arbitrary"),
                                2 * _nbytes((tm, tk), BF16), 2 * _nbytes((tk, D), BF16),
                                5 * _nbytes((tm, D), F32)),
        name="ffn_down_norm",
    )(act, w, x1, g.reshape(1, D))


def _layer(x, l, g_pre_mix, w_in, b_gates, m_conv_w, m_conv_b, m_head_norm,
           lambda_q1, lambda_k1, lambda_q2, lambda_k2, a_head_norm,
           w_branch_m, w_branch_a, w_out, g_post_mix, g_pre_ffn,
           w_up, ffn_conv_w, ffn_conv_b, w_down, g_post_ffn, batch, seq_len):
    D = x.shape[1]
    mw = M_HEADS * M_HEAD_DIM
    aw = A_HEADS * A_V_DIM
    sizes = (2 * mw, mw, mw, 2 * M_HEADS, aw, aw, aw, 2 * D)
    o_qk, o_v, o_o, o_if, o_aq, o_ak, o_av, o_mg, o_end = [int(s) for s in np.cumsum((0,) + sizes)]
    lam_init = 0.8 - 0.6 * math.exp(-0.3 * l)

    w_conv = w_in[:, o_qk:o_v].astype(BF16)
    w_plain = jnp.concatenate([w_in[:, o_v:o_o], w_in[:, o_aq:o_mg]], axis=1).astype(BF16)
    w_sig = jnp.concatenate([w_in[:, o_o:o_if], w_in[:, o_mg:o_end]], axis=1).astype(BF16)
    n_gate = 2 * M_HEADS
    w_gate = jnp.pad(w_in[:, o_if:o_aq], ((0, 0), (0, V7X_LANES - n_gate))).astype(BF16)
    b_gate = jnp.pad(b_gates.astype(F32), (0, V7X_LANES - n_gate))
    q_scale = jnp.concatenate([jnp.full((mw,), M_HEAD_DIM ** -0.5, F32), jnp.ones((mw,), F32)])
    lam_vecs = jnp.stack([lambda_q1, lambda_k1, lambda_q2, lambda_k2]).astype(F32)

    h = _rmsnorm(x, g_pre_mix)
    qk = _proj_conv_silu(h, w_conv, m_conv_w, m_conv_b, q_scale, seq_len)
    plain = _proj_act(h, w_plain, jnp.zeros((w_plain.shape[1],), F32), "identity", BF16)
    sig = _proj_act(h, w_sig, jnp.zeros((w_sig.shape[1],), F32), "sigmoid", BF16)
    gates = _proj_act(h, w_gate, b_gate, "identity", F32)

    hm = _mlstm(qk, plain, sig, gates, m_head_norm, batch, seq_len)
    ha = _diff_attention(plain, lam_vecs, a_head_norm, lam_init, batch, seq_len)

    mix = _branch_merge(hm, ha, w_branch_m.astype(BF16), w_branch_a.astype(BF16), sig)
    x1, h2 = _outproj(mix, w_out.astype(BF16), x, g_post_mix, g_pre_ffn)
    act = _ffn_up(h2, w_up.astype(BF16), ffn_conv_w, ffn_conv_b, seq_len)
    return _ffn_down(act, w_down.astype(BF16), x1, g_post_ffn)


def kernel(x, g_pre_mix, w_in, b_gates, m_conv_w, m_conv_b, m_head_norm, lambda_q1, lambda_k1, lambda_q2, lambda_k2, a_head_norm, w_branch_m, w_branch_a, w_out, g_post_mix, g_pre_ffn, w_up, ffn_conv_w, ffn_conv_b, w_down, g_post_ffn):
    B, S, D = x.shape
    layers = (g_pre_mix, w_in, b_gates, m_conv_w, m_conv_b, m_head_norm, lambda_q1, lambda_k1,
              lambda_q2, lambda_k2, a_head_norm, w_branch_m, w_branch_a, w_out, g_post_mix,
              g_pre_ffn, w_up, ffn_conv_w, ffn_conv_b, w_down, g_post_ffn)
    y = x.reshape(B * S, D)
    for l in range(w_in.shape[0]):
        y = _layer(y, l, *[p[l] for p in layers], batch=B, seq_len=S)
    return y.reshape(B, S, D)
```

```python
import functools
import math

import numpy as np
import jax
import jax.numpy as jnp
from jax import lax
from jax.experimental import pallas as pl
from jax.experimental.pallas import tpu as pltpu

F32 = jnp.float32
BF16 = jnp.bfloat16

EPS = 1e-6
CHUNK = 64
M_HEADS = 8
M_HEAD_DIM = 256
A_HEADS = 8
A_QK_DIM = 128
A_V_DIM = 256
ALIBI_SLOPES = 2.0 ** (-8.0 * np.arange(1, A_HEADS + 1) / A_HEADS)

V7X_LANES = 128
V7X_SUBLANES = 8
V7X_MXU_WIDTH = 256
V7X_NUM_MXU = 2
V7X_SCOPED_VMEM_BYTES = 60000 * 1024
COMPILER_TEMP_BYTES = 16 * 1024 * 1024

MLSTM_CHUNK = V7X_MXU_WIDTH
ATTN_TILE = 512
MASK_VALUE = -1e30
LOG2E = math.log2(math.e)


def _vmem_limit(*buffer_bytes):
    return int(min(sum(buffer_bytes) + COMPILER_TEMP_BYTES, V7X_SCOPED_VMEM_BYTES))


def _nbytes(shape, dtype):
    return int(np.prod(shape)) * jnp.dtype(dtype).itemsize


def _params(semantics, *buffer_bytes):
    return pltpu.CompilerParams(dimension_semantics=semantics,
                                vmem_limit_bytes=_vmem_limit(*buffer_bytes))


def _rms_scale(y):
    return lax.rsqrt(jnp.mean(y * y, axis=-1, keepdims=True) + EPS)


def _col_blocks(width):
    sub = V7X_NUM_MXU * V7X_MXU_WIDTH
    if width % sub:
        return [slice(0, width)]
    return [slice(c, c + sub) for c in range(0, width, sub)]


def _rmsnorm_kernel(x_ref, g_ref, o_ref):
    x = x_ref[...]
    o_ref[...] = (x * _rms_scale(x) * g_ref[...]).astype(o_ref.dtype)


def _rmsnorm(x, g, tm=512):
    T, D = x.shape
    return pl.pallas_call(
        _rmsnorm_kernel,
        grid=(T // tm,),
        in_specs=[pl.BlockSpec((tm, D), lambda i: (i, 0)),
                  pl.BlockSpec((1, D), lambda i: (0, 0))],
        out_specs=pl.BlockSpec((tm, D), lambda i: (i, 0)),
        out_shape=jax.ShapeDtypeStruct((T, D), BF16),
        compiler_params=_params(("parallel",), 2 * _nbytes((tm, D), F32), 2 * _nbytes((tm, D), BF16)),
        name="rmsnorm",
    )(x, g.reshape(1, D))


def _load_conv_halo(buf_ref, carry_ref, first_tile):
    halo = V7X_SUBLANES

    @pl.when(first_tile)
    def _():
        buf_ref[0:halo, :] = jnp.zeros((halo, buf_ref.shape[1]), F32)

    @pl.when(jnp.logical_not(first_tile))
    def _():
        buf_ref[0:halo, :] = carry_ref[...]


def _causal_conv_rows(acc, cols, buf_ref, carry_ref, cw_ref, cb_ref, taps):
    tm = acc.shape[0]
    halo = V7X_SUBLANES
    buf_ref[halo:halo + tm, cols] = acc
    carry_ref[:, cols] = acc[tm - halo:tm, :]
    y = cb_ref[:, cols] + cw_ref[taps - 1:taps, cols] * acc
    for k in range(taps - 1):
        off = halo - (taps - 1) + k
        y = y + cw_ref[k:k + 1, cols] * buf_ref[off:off + tm, cols]
    return y


def _proj_conv_silu_kernel(h_ref, w_ref, cw_ref, cb_ref, sc_ref, o_ref, carry_ref, buf_ref, *,
                           tiles_per_seq, taps):
    i = pl.program_id(0)
    j = pl.program_id(1)
    _load_conv_halo(buf_ref, carry_ref.at[j], (i % tiles_per_seq) == 0)
    for cols in _col_blocks(o_ref.shape[1]):
        acc = jnp.dot(h_ref[...], w_ref[:, cols], preferred_element_type=F32)
        y = _causal_conv_rows(acc, cols, buf_ref, carry_ref.at[j], cw_ref, cb_ref, taps)
        y = y * jax.nn.sigmoid(y)
        o_ref[:, cols] = (y * sc_ref[:, cols]).astype(o_ref.dtype)


def _proj_conv_silu(h, w, n_cols, cw, cb, scale_row, seq_len, tm=1024, tn=1024):
    T, K = h.shape
    taps = cw.shape[0]
    nj = n_cols // tn
    kern = functools.partial(_proj_conv_silu_kernel, tiles_per_seq=seq_len // tm, taps=taps)
    return pl.pallas_call(
        kern,
        grid=(T // tm, nj),
        in_specs=[pl.BlockSpec((tm, K), lambda i, j: (i, 0)),
                  pl.BlockSpec((K, tn), lambda i, j: (0, j)),
                  pl.BlockSpec((taps, tn), lambda i, j: (0, j)),
                  pl.BlockSpec((1, tn), lambda i, j: (0, j)),
                  pl.BlockSpec((1, tn), lambda i, j: (0, j))],
        out_specs=pl.BlockSpec((tm, tn), lambda i, j: (i, j)),
        out_shape=jax.ShapeDtypeStruct((T, n_cols), BF16),
        scratch_shapes=[pltpu.VMEM((nj, V7X_SUBLANES, tn), F32),
                        pltpu.VMEM((tm + V7X_SUBLANES, tn), F32)],
        compiler_params=_params(("arbitrary", "arbitrary"),
                                2 * _nbytes((tm, K), BF16), 2 * _nbytes((K, tn), BF16),
                                2 * _nbytes((tm, tn), BF16), 2 * _nbytes((tm, tn), F32)),
        name="proj_conv_silu",
    )(h, w, cw, cb.reshape(1, n_cols), scale_row.reshape(1, n_cols))


def _proj_act_kernel(h_ref, w_ref, b_ref, o_ref, *, act):
    for cols in _col_blocks(o_ref.shape[1]):
        acc = jnp.dot(h_ref[...], w_ref[:, cols], preferred_element_type=F32) + b_ref[:, cols]
        if act == "sigmoid":
            acc = jax.nn.sigmoid(acc)
        o_ref[:, cols] = acc.astype(o_ref.dtype)


def _proj_act(h, w, col0, n_cols, bias, act, out_dtype, tm=1024, tn=1024):
    T, K = h.shape
    tn = min(tn, n_cols)
    j0 = col0 // tn
    return pl.pallas_call(
        functools.partial(_proj_act_kernel, act=act),
        grid=(T // tm, n_cols // tn),
        in_specs=[pl.BlockSpec((tm, K), lambda i, j: (i, 0)),
                  pl.BlockSpec((K, tn), lambda i, j: (0, j0 + j)),
                  pl.BlockSpec((1, tn), lambda i, j: (0, j))],
        out_specs=pl.BlockSpec((tm, tn), lambda i, j: (i, j)),
        out_shape=jax.ShapeDtypeStruct((T, n_cols), out_dtype),
        compiler_params=_params(("parallel", "parallel"),
                                2 * _nbytes((tm, K), BF16), 2 * _nbytes((K, tn), BF16),
                                2 * _nbytes((tm, tn), out_dtype), _nbytes((tm, tn), F32)),
        name="proj_" + act,
    )(h, w, bias.reshape(1, n_cols))


def _mlstm_kernel(q_ref, k_ref, v_ref, og_ref, g_ref, gain_ref, o_ref, c_ref, n_ref, m_ref, *,
                  heads, head_dim):
    L = q_ref.shape[0]
    H, Dh = heads, head_dim

    @pl.when(pl.program_id(1) == 0)
    def _():
        c_ref[...] = jnp.zeros_like(c_ref)
        n_ref[...] = jnp.zeros_like(n_ref)
        m_ref[...] = jnp.zeros_like(m_ref)

    gates = g_ref[...]
    logf = jnp.minimum(gates, 0.0) - jnp.log1p(jnp.exp(-jnp.abs(gates)))
    row = lax.broadcasted_iota(jnp.int32, (L, L), 0)
    col = lax.broadcasted_iota(jnp.int32, (L, L), 1)
    tri = col <= row
    cum = jnp.dot(tri.astype(F32), logf, precision=lax.Precision.HIGHEST,
                  preferred_element_type=F32)
    gates_t = gates.T
    cum_t = cum.T

    for h in range(H):
        hs = slice(h * Dh, (h + 1) * Dh)
        b_col = cum[:, H + h:H + h + 1]
        u_col = gates[:, h:h + 1] - b_col
        u_row = gates_t[h:h + 1, :] - cum_t[H + h:H + h + 1, :]
        m_prev = m_ref[h, 0:1, 0:1]
        cu = jnp.max(jnp.where(tri, u_row, -jnp.inf), axis=1, keepdims=True)
        mm = jnp.maximum(m_prev, cu)
        w_intra = jnp.where(tri, jnp.exp(u_row - mm), 0.0)
        w_inter = jnp.exp(m_prev - mm)

        qh = q_ref[:, hs]
        kh = k_ref[:, hs]
        vh = v_ref[:, hs]
        s = lax.dot_general(qh, kh, (((1,), (1,)), ((), ())), preferred_element_type=F32)
        sqk = s * w_intra
        num = (jnp.dot(sqk.astype(BF16), vh, preferred_element_type=F32)
               + w_inter * jnp.dot(qh, c_ref[h].astype(BF16), preferred_element_type=F32))
        qn = jnp.sum(qh.astype(F32) * n_ref[h], axis=1, keepdims=True)
        den = jnp.sum(sqk, axis=1, keepdims=True) + w_inter * qn
        hh = num / jnp.maximum(jnp.abs(den), jnp.exp(-(b_col + mm)))
        y = hh * _rms_scale(hh) * gain_ref[:, hs]
        o_ref[:, hs] = (y * og_ref[:, hs].astype(F32)).astype(o_ref.dtype)

        mm_last = mm[L - 1:L, :]
        ws = jnp.exp(u_col - mm_last)
        decay = jnp.exp(m_prev - mm_last)
        kw = kh.astype(F32) * ws
        c_ref[h] = decay * c_ref[h] + lax.dot_general(
            kw.astype(BF16), vh, (((0,), (0,)), ((), ())), preferred_element_type=F32)
        n_ref[h] = decay * n_ref[h] + jnp.sum(kw, axis=0, keepdims=True)
        m_ref[h] = jnp.broadcast_to(b_col[L - 1:L, :] + mm_last, m_ref.shape[1:])


def _mlstm(qk, v, og, gates, gain, batch, seq_len):
    T = qk.shape[0]
    H, Dh, L = M_HEADS, M_HEAD_DIM, MLSTM_CHUNK
    W = H * Dh
    nt = seq_len // L
    row_blk = lambda b, t: (b * nt + t, 0)
    return pl.pallas_call(
        functools.partial(_mlstm_kernel, heads=H, head_dim=Dh),
        grid=(batch, nt),
        in_specs=[pl.BlockSpec((L, W), row_blk),
                  pl.BlockSpec((L, W), lambda b, t: (b * nt + t, 1)),
                  pl.BlockSpec((L, W), row_blk),
                  pl.BlockSpec((L, W), row_blk),
                  pl.BlockSpec((L, V7X_LANES), row_blk),
                  pl.BlockSpec((1, W), lambda b, t: (0, 0))],
        out_specs=pl.BlockSpec((L, W), row_blk),
        out_shape=jax.ShapeDtypeStruct((T, W), BF16),
        scratch_shapes=[pltpu.VMEM((H, Dh, Dh), F32),
                        pltpu.VMEM((H, 1, Dh), F32),
                        pltpu.VMEM((H, V7X_SUBLANES, V7X_LANES), F32)],
        compiler_params=_params(("arbitrary", "arbitrary"),
                                10 * _nbytes((L, W), BF16), _nbytes((H, Dh, Dh), F32)),
        name="mlstm",
    )(qk, qk, v, og, gates, gain.reshape(1, W))


def _attn_kernel(slopes_ref, lam_ref, q_ref, k_ref, v_ref, gain_ref, o_ref,
                 vt_ref, acc_ref, m_ref, l_ref, s_ref, p_ref, al_ref, *, tile, lam_init):
    h = pl.program_id(1)
    qi = pl.program_id(2)
    dk = A_QK_DIM
    slope = slopes_ref[h] * LOG2E
    seq = k_ref.shape[0]

    @pl.when(qi == 0)
    def _():
        for c in range(seq // tile):
            vt_ref[:, c * tile:(c + 1) * tile] = v_ref[c * tile:(c + 1) * tile, :].T

    m_ref[...] = jnp.full(m_ref.shape, MASK_VALUE, F32)
    l_ref[...] = jnp.zeros_like(l_ref)
    acc_ref[...] = jnp.zeros_like(acc_ref)
    p_ref[1] = jnp.zeros(p_ref.shape[1:], BF16)
    al_ref[1] = jnp.ones(al_ref.shape[1:], F32)

    krow = lax.broadcasted_iota(jnp.int32, (tile, tile), 0)
    qcol = lax.broadcasted_iota(jnp.int32, (tile, tile), 1)
    rel = (qcol - krow).astype(F32)
    bias_full = -slope * rel

    def scores(j, slot):
        kt = k_ref[pl.ds(pl.multiple_of(j * tile, tile), tile), :]
        for c in range(2):
            s_ref[slot, c] = lax.dot_general(kt[:, c * dk:(c + 1) * dk], q_ref[:, c * dk:(c + 1) * dk],
                                             (((1,), (1,)), ((), ())), preferred_element_type=F32)

    def softmax_step(slot, bias, shift):
        for c in range(2):
            sc = s_ref[slot, c] + bias
            m_prev = m_ref[c]
            m_new = jnp.maximum(m_prev, jnp.max(sc, axis=0, keepdims=True) + shift)
            alpha = jnp.exp2(m_prev - m_new)
            p = jnp.exp2(sc - (m_new - shift))
            l_ref[c] = alpha * l_ref[c] + jnp.sum(p, axis=0, keepdims=True)
            m_ref[c] = m_new
            p_ref[slot, c] = p.astype(BF16)
            al_ref[slot, c] = alpha

    def accumulate(j, slot):
        vt = vt_ref[:, pl.ds(pl.multiple_of(j * tile, tile), tile)]
        for c in range(2):
            acc_ref[c] = al_ref[slot, c] * acc_ref[c] + jnp.dot(vt, p_ref[slot, c], preferred_element_type=F32)

    def shift_of(j):
        return -slope * ((qi - j) * tile).astype(F32)

    first = lax.rem(qi, 2)
    pairs = lax.div(qi, 2)

    @pl.when(first == 1)
    def _():
        scores(0, 0)
        softmax_step(0, bias_full, shift_of(0))
        accumulate(0, 0)

    scores(first, 0)

    def pair(g, carry):
        a = first + 2 * g
        accumulate(jnp.maximum(a - 1, 0), 1)
        softmax_step(0, bias_full, shift_of(a))
        scores(a + 1, 1)
        accumulate(a, 0)
        softmax_step(1, bias_full, shift_of(a + 1))
        scores(a + 2, 0)
        return carry

    lax.fori_loop(0, pairs, pair, 0)

    accumulate(jnp.maximum(qi - 1, 0), 1)
    allowed = (krow // CHUNK) <= (qcol // CHUNK)
    softmax_step(0, jnp.where(allowed, -slope * jnp.abs(rel), MASK_VALUE), 0.0)
    accumulate(qi, 0)

    lam_vec = lam_ref[...]
    lam = (jnp.exp(jnp.sum(lam_vec[0:1] * lam_vec[1:2], axis=1, keepdims=True))
           - jnp.exp(jnp.sum(lam_vec[2:3] * lam_vec[3:4], axis=1, keepdims=True)) + lam_init)
    o = acc_ref[0] / l_ref[0] - lam * (acc_ref[1] / l_ref[1])
    scale = lax.rsqrt(jnp.mean(o * o, axis=0, keepdims=True) + EPS) * (1.0 - lam_init)
    o_ref[...] = ((o * scale).T * gain_ref[...]).astype(o_ref.dtype)


def _diff_attention(a, lam_vecs, gain, lam_init, batch, seq_len):
    T = a.shape[0]
    H, Dv = A_HEADS, A_V_DIM
    W = H * Dv
    tile = ATTN_TILE
    assert tile % CHUNK == 0 and seq_len % tile == 0
    nq = seq_len // tile
    slopes = jnp.asarray(ALIBI_SLOPES, F32)
    return pl.pallas_call(
        functools.partial(_attn_kernel, tile=tile, lam_init=lam_init),
        grid=(batch, H, nq),
        in_specs=[pl.BlockSpec(memory_space=pltpu.SMEM),
                  pl.BlockSpec((4, A_QK_DIM), lambda b, h, i: (0, 0)),
                  pl.BlockSpec((tile, Dv), lambda b, h, i: (b * nq + i, h)),
                  pl.BlockSpec((seq_len, Dv), lambda b, h, i: (b, H + h)),
                  pl.BlockSpec((seq_len, Dv), lambda b, h, i: (b, 2 * H + h)),
                  pl.BlockSpec((1, Dv), lambda b, h, i: (0, h))],
        out_specs=pl.BlockSpec((tile, Dv), lambda b, h, i: (b * nq + i, h)),
        out_shape=jax.ShapeDtypeStruct((T, W), BF16),
        scratch_shapes=[pltpu.VMEM((Dv, seq_len), BF16),
                        pltpu.VMEM((2, Dv, tile), F32),
                        pltpu.VMEM((2, 1, tile), F32),
                        pltpu.VMEM((2, 1, tile), F32),
                        pltpu.VMEM((2, 2, tile, tile), F32),
                        pltpu.VMEM((2, 2, tile, tile), BF16),
                        pltpu.VMEM((2, 2, 1, tile), F32)],
        compiler_params=_params(("arbitrary", "arbitrary", "arbitrary"),
                                5 * _nbytes((seq_len, Dv), BF16), 4 * _nbytes((tile, Dv), BF16),
                                2 * _nbytes((Dv, tile), F32), 4 * _nbytes((tile, tile), F32),
                                4 * _nbytes((tile, tile), BF16)),
        name="diff_attention",
    )(slopes, lam_vecs, a, a, a, gain.reshape(1, W))


def _merge_kernel(am_ref, aa_ref, wm_ref, wa_ref, gm_ref, ga_ref, o_ref):
    for cols in _col_blocks(o_ref.shape[1]):
        ym = jnp.dot(am_ref[...], wm_ref[:, cols], preferred_element_type=F32)
        ya = jnp.dot(aa_ref[...], wa_ref[:, cols], preferred_element_type=F32)
        o_ref[:, cols] = (gm_ref[:, cols].astype(F32) * ym + ga_ref[:, cols].astype(F32) * ya).astype(o_ref.dtype)


def _branch_merge(hm, ha, wm, wa, g, tm=1024, tn=1024):
    T, K = hm.shape
    N = wm.shape[1]
    nj = N // tn
    return pl.pallas_call(
        _merge_kernel,
        grid=(T // tm, nj),
        in_specs=[pl.BlockSpec((tm, K), lambda i, j: (i, 0)),
                  pl.BlockSpec((tm, K), lambda i, j: (i, 0)),
                  pl.BlockSpec((K, tn), lambda i, j: (0, j)),
                  pl.BlockSpec((K, tn), lambda i, j: (0, j)),
                  pl.BlockSpec((tm, tn), lambda i, j: (i, j)),
                  pl.BlockSpec((tm, tn), lambda i, j: (i, nj + j))],
        out_specs=pl.BlockSpec((tm, tn), lambda i, j: (i, j)),
        out_shape=jax.ShapeDtypeStruct((T, N), BF16),
        compiler_params=_params(("parallel", "parallel"),
                                4 * _nbytes((tm, K), BF16), 4 * _nbytes((K, tn), BF16),
                                6 * _nbytes((tm, tn), BF16), 2 * _nbytes((tm, tn), F32)),
        name="branch_merge",
    )(hm, ha, wm, wa, g, g)


def _outproj_kernel(mix_ref, w_ref, x_ref, gpost_ref, gpre_ref, x1_ref, h2_ref):
    y = jnp.dot(mix_ref[...], w_ref[...], preferred_element_type=F32)
    x1 = x_ref[...] + y * _rms_scale(y) * gpost_ref[...]
    x1_ref[...] = x1
    h2_ref[...] = (x1 * _rms_scale(x1) * gpre_ref[...]).astype(h2_ref.dtype)


def _outproj(mix, w, x, g_post, g_pre, tm=512):
    T, K = mix.shape
    D = w.shape[1]
    row = lambda i: (i, 0)
    fixed = lambda i: (0, 0)
    return pl.pallas_call(
        _outproj_kernel,
        grid=(T // tm,),
        in_specs=[pl.BlockSpec((tm, K), row),
                  pl.BlockSpec((K, D), fixed),
                  pl.BlockSpec((tm, D), row),
                  pl.BlockSpec((1, D), fixed),
                  pl.BlockSpec((1, D), fixed)],
        out_specs=[pl.BlockSpec((tm, D), row), pl.BlockSpec((tm, D), row)],
        out_shape=[jax.ShapeDtypeStruct((T, D), F32), jax.ShapeDtypeStruct((T, D), BF16)],
        compiler_params=_params(("parallel",),
                                2 * _nbytes((tm, K), BF16), 2 * _nbytes((K, D), BF16),
                                4 * _nbytes((tm, D), F32), 2 * _nbytes((tm, D), BF16),
                                2 * _nbytes((tm, D), F32)),
        name="outproj_norm",
    )(mix, w, x, g_post.reshape(1, D), g_pre.reshape(1, D))


def _gelu_tanh(x):
    return 0.5 * x * (1.0 + jnp.tanh(math.sqrt(2.0 / math.pi) * (x + 0.044715 * (x * x * x))))


def _ffn_up_kernel(h_ref, wg_ref, wv_ref, cwg_ref, cwv_ref, cbg_ref, cbv_ref, o_ref,
                   carry_ref, buf_ref, *, tiles_per_seq, taps):
    i = pl.program_id(0)
    j = pl.program_id(1)
    first = (i % tiles_per_seq) == 0
    _load_conv_halo(buf_ref.at[0], carry_ref.at[j, 0], first)
    _load_conv_halo(buf_ref.at[1], carry_ref.at[j, 1], first)
    for cols in _col_blocks(o_ref.shape[1]):
        gate = _causal_conv_rows(jnp.dot(h_ref[...], wg_ref[:, cols], preferred_element_type=F32), cols,
                                 buf_ref.at[0], carry_ref.at[j, 0], cwg_ref, cbg_ref, taps)
        val = _causal_conv_rows(jnp.dot(h_ref[...], wv_ref[:, cols], preferred_element_type=F32), cols,
                                buf_ref.at[1], carry_ref.at[j, 1], cwv_ref, cbv_ref, taps)
        o_ref[:, cols] = (_gelu_tanh(gate) * val).astype(o_ref.dtype)


def _ffn_up(h2, w_up, cw, cb, seq_len, tm=1024, tf=512):
    T, K = h2.shape
    F = w_up.shape[1] // 2
    taps = cw.shape[0]
    nj = F // tf
    gate_col = lambda i, j: (0, j)
    val_col = lambda i, j: (0, nj + j)
    cb2 = cb.reshape(1, 2 * F)
    return pl.pallas_call(
        functools.partial(_ffn_up_kernel, tiles_per_seq=seq_len // tm, taps=taps),
        grid=(T // tm, nj),
        in_specs=[pl.BlockSpec((tm, K), lambda i, j: (i, 0)),
                  pl.BlockSpec((K, tf), gate_col),
                  pl.BlockSpec((K, tf), val_col),
                  pl.BlockSpec((taps, tf), gate_col),
                  pl.BlockSpec((taps, tf), val_col),
                  pl.BlockSpec((1, tf), gate_col),
                  pl.BlockSpec((1, tf), val_col)],
        out_specs=pl.BlockSpec((tm, tf), lambda i, j: (i, j)),
        out_shape=jax.ShapeDtypeStruct((T, F), BF16),
        scratch_shapes=[pltpu.VMEM((nj, 2, V7X_SUBLANES, tf), F32),
                        pltpu.VMEM((2, tm + V7X_SUBLANES, tf), F32)],
        compiler_params=_params(("arbitrary", "arbitrary"),
                                2 * _nbytes((tm, K), BF16), 4 * _nbytes((K, tf), BF16),
                                2 * _nbytes((tm, tf), BF16), 4 * _nbytes((tm, tf), F32)),
        name="ffn_up_conv_gelu",
    )(h2, w_up, w_up, cw, cw, cb2, cb2)


def _ffn_down_kernel(a_ref, w_ref, x1_ref, g_ref, o_ref, acc_ref):
    k = pl.program_id(1)

    @pl.when(k == 0)
    def _():
        acc_ref[...] = jnp.zeros_like(acc_ref)

    for cols in _col_blocks(acc_ref.shape[1]):
        acc_ref[:, cols] += jnp.dot(a_ref[...], w_ref[:, cols], preferred_element_type=F32)

    @pl.when(k == pl.num_programs(1) - 1)
    def _():
        y = acc_ref[...]
        o_ref[...] = x1_ref[...] + y * _rms_scale(y) * g_ref[...]


def _ffn_down(act, w, x1, g, tm=1024, tk=512):
    T, F = act.shape
    D = w.shape[1]
    return pl.pallas_call(
        _ffn_down_kernel,
        grid=(T // tm, F // tk),
        in_specs=[pl.BlockSpec((tm, tk), lambda i, k: (i, k)),
                  pl.BlockSpec((tk, D), lambda i, k: (k, 0)),
                  pl.BlockSpec((tm, D), lambda i, k: (i, 0)),
                  pl.BlockSpec((1, D), lambda i, k: (0, 0))],
        out_specs=pl.BlockSpec((tm, D), lambda i, k: (i, 0)),
        out_shape=jax.ShapeDtypeStruct((T, D), F32),
        scratch_shapes=[pltpu.VMEM((tm, D), F32)],
        compiler_params=_params(("parallel", "arbitrary"),
                                2 * _nbytes((tm, tk), BF16), 2 * _nbytes((tk, D), BF16),
                                5 * _nbytes((tm, D), F32)),
        name="ffn_down_norm",
    )(act, w, x1, g.reshape(1, D))


def _layer(x, l, g_pre_mix, w_in, b_gates, m_conv_w, m_conv_b, m_head_norm,
           lambda_q1, lambda_k1, lambda_q2, lambda_k2, a_head_norm,
           w_branch_m, w_branch_a, w_out, g_post_mix, g_pre_ffn,
           w_up, ffn_conv_w, ffn_conv_b, w_down, g_post_ffn, batch, seq_len):
    D = x.shape[1]
    mw = M_HEADS * M_HEAD_DIM
    aw = A_HEADS * A_V_DIM
    n_gate = 2 * M_HEADS
    pre = 4 * mw
    post0 = pre + n_gate
    lam_init = 0.8 - 0.6 * math.exp(-0.3 * l)

    w_pre = w_in[:, :pre].astype(BF16)
    q_cols = jnp.concatenate([jnp.full((aw,), A_QK_DIM ** -0.5 * LOG2E, F32),
                              jnp.ones((2 * aw + 2 * D,), F32)])
    w_post = (w_in[:, post0:] * q_cols).astype(BF16)
    w_gate = jnp.pad(w_in[:, pre:post0], ((0, 0), (0, V7X_LANES - n_gate))).astype(BF16)
    b_gate = jnp.pad(b_gates.astype(F32), (0, V7X_LANES - n_gate))
    mq_scale = jnp.concatenate([jnp.full((mw,), M_HEAD_DIM ** -0.5, F32), jnp.ones((mw,), F32)])
    lam_vecs = jnp.stack([lambda_q1, lambda_k1, lambda_q2, lambda_k2]).astype(F32)
    no_bias = lambda n: jnp.zeros((n,), F32)

    h = _rmsnorm(x, g_pre_mix)
    qk = _proj_conv_silu(h, w_pre, 2 * mw, m_conv_w, m_conv_b, mq_scale, seq_len)
    v = _proj_act(h, w_pre, 2 * mw, mw, no_bias(mw), "identity", BF16)
    og = _proj_act(h, w_pre, 3 * mw, mw, no_bias(mw), "sigmoid", BF16)
    gates = _proj_act(h, w_gate, 0, V7X_LANES, b_gate, "identity", F32)
    a = _proj_act(h, w_post, 0, 3 * aw, no_bias(3 * aw), "identity", BF16)
    g = _proj_act(h, w_post, 3 * aw, 2 * D, no_bias(2 * D), "sigmoid", BF16)

    hm = _mlstm(qk, v, og, gates, m_head_norm, batch, seq_len)
    ha = _diff_attention(a, lam_vecs, a_head_norm, lam_init, batch, seq_len)

    mix = _branch_merge(hm, ha, w_branch_m.astype(BF16), w_branch_a.astype(BF16), g)
    x1, h2 = _outproj(mix, w_out.astype(BF16), x, g_post_mix, g_pre_ffn)
    act = _ffn_up(h2, w_up.astype(BF16), ffn_conv_w, ffn_conv_b, seq_len)
    return _ffn_down(act, w_down.astype(BF16), x1, g_post_ffn)


def kernel(x, g_pre_mix, w_in, b_gates, m_conv_w, m_conv_b, m_head_norm, lambda_q1, lambda_k1, lambda_q2, lambda_k2, a_head_norm, w_branch_m, w_branch_a, w_out, g_post_mix, g_pre_ffn, w_up, ffn_conv_w, ffn_conv_b, w_down, g_post_ffn):
    B, S, D = x.shape
    layers = (g_pre_mix, w_in, b_gates, m_conv_w, m_conv_b, m_head_norm, lambda_q1, lambda_k1,
              lambda_q2, lambda_k2, a_head_norm, w_branch_m, w_branch_a, w_out, g_post_mix,
              g_pre_ffn, w_up, ffn_conv_w, ffn_conv_b, w_down, g_post_ffn)
    y = x.reshape(B * S, D)
    for l in range(w_in.shape[0]):
        y = _layer(y, l, *[p[l] for p in layers], batch=B, seq_len=S)
    return y.reshape(B, S, D)
```

```python
import functools
import math

import numpy as np
import jax
import jax.numpy as jnp
from jax import lax
from jax.experimental import pallas as pl
from jax.experimental.pallas import tpu as pltpu

F32 = jnp.float32
BF16 = jnp.bfloat16

EPS = 1e-6
CHUNK = 64
M_HEADS = 8
M_HEAD_DIM = 256
A_HEADS = 8
A_QK_DIM = 128
A_V_DIM = 256
ALIBI_SLOPES = 2.0 ** (-8.0 * np.arange(1, A_HEADS + 1) / A_HEADS)

V7X_LANES = 128
V7X_SUBLANES = 8
V7X_MXU_WIDTH = 256
V7X_NUM_MXU = 2
V7X_SCOPED_VMEM_BYTES = 60000 * 1024
COMPILER_TEMP_BYTES = 16 * 1024 * 1024

MLSTM_CHUNK = V7X_MXU_WIDTH
ATTN_TILE = 512
ATTN_VT_PAD = 16
UNDERFLOW_EXP2 = 160.0
ATTN_BOUND_MARGIN = 1.0
MASK_VALUE = -1e30
LOG2E = math.log2(math.e)


def _vmem_limit(*buffer_bytes):
    return int(min(sum(buffer_bytes) + COMPILER_TEMP_BYTES, V7X_SCOPED_VMEM_BYTES))


def _nbytes(shape, dtype):
    return int(np.prod(shape)) * jnp.dtype(dtype).itemsize


def _params(semantics, *buffer_bytes):
    return pltpu.CompilerParams(dimension_semantics=semantics,
                                vmem_limit_bytes=_vmem_limit(*buffer_bytes))


def _rms_scale(y):
    return lax.rsqrt(jnp.mean(y * y, axis=-1, keepdims=True) + EPS)


def _col_blocks(width):
    sub = V7X_NUM_MXU * V7X_MXU_WIDTH
    if width % sub:
        return [slice(0, width)]
    return [slice(c, c + sub) for c in range(0, width, sub)]


def _rmsnorm_kernel(x_ref, g_ref, o_ref):
    x = x_ref[...]
    o_ref[...] = (x * _rms_scale(x) * g_ref[...]).astype(o_ref.dtype)


def _rmsnorm(x, g, tm=512):
    T, D = x.shape
    return pl.pallas_call(
        _rmsnorm_kernel,
        grid=(T // tm,),
        in_specs=[pl.BlockSpec((tm, D), lambda i: (i, 0)),
                  pl.BlockSpec((1, D), lambda i: (0, 0))],
        out_specs=pl.BlockSpec((tm, D), lambda i: (i, 0)),
        out_shape=jax.ShapeDtypeStruct((T, D), BF16),
        compiler_params=_params(("parallel",), 2 * _nbytes((tm, D), F32), 2 * _nbytes((tm, D), BF16)),
        name="rmsnorm",
    )(x, g.reshape(1, D))


def _cast_cols_kernel(w_ref, o_ref):
    o_ref[...] = w_ref[...].astype(o_ref.dtype)


def _cast_cols(w, n_cols, tr=256):
    K = w.shape[0]
    return pl.pallas_call(
        _cast_cols_kernel,
        grid=(K // tr,),
        in_specs=[pl.BlockSpec((tr, n_cols), lambda i: (i, 0))],
        out_specs=pl.BlockSpec((tr, n_cols), lambda i: (i, 0)),
        out_shape=jax.ShapeDtypeStruct((K, n_cols), BF16),
        compiler_params=_params(("parallel",), 2 * _nbytes((tr, n_cols), F32), 2 * _nbytes((tr, n_cols), BF16)),
        name="cast_cols",
    )(w)


def _shift_cast_kernel(main_ref, next_ref, sc_main_ref, sc_next_ref, o_ref, *, lead):
    tn = main_ref.shape[1]
    nxt = next_ref[...] * sc_next_ref[...]
    lane = lax.broadcasted_iota(jnp.int32, nxt.shape, 1)
    nxt = jnp.where(lane < lead, nxt, 0.0)
    window = jnp.concatenate([main_ref[...] * sc_main_ref[...], nxt], axis=1).astype(BF16)
    src = lax.broadcasted_iota(jnp.int32, (window.shape[1], tn), 0)
    dst = lax.broadcasted_iota(jnp.int32, (window.shape[1], tn), 1)
    select = jnp.where(src == dst + lead, 1.0, 0.0).astype(BF16)
    o_ref[...] = jnp.dot(window, select, preferred_element_type=F32).astype(o_ref.dtype)


def _shift_cast_cols(w, col0, lead, n_cols, in_scale, tr=512, tn=1024):
    K = w.shape[0]
    assert col0 % tn == 0 and 0 < lead < V7X_LANES and n_cols % tn == 0
    j0 = col0 // tn
    per = tn // V7X_LANES
    pad = n_cols + V7X_LANES - in_scale.shape[0]
    sc = jnp.pad(in_scale.astype(F32), (0, pad)).reshape(1, n_cols + V7X_LANES)
    return pl.pallas_call(
        functools.partial(_shift_cast_kernel, lead=lead),
        grid=(K // tr, n_cols // tn),
        in_specs=[pl.BlockSpec((tr, tn), lambda i, j: (i, j0 + j)),
                  pl.BlockSpec((tr, V7X_LANES), lambda i, j: (i, (j0 + j + 1) * per)),
                  pl.BlockSpec((1, tn), lambda i, j: (0, j)),
                  pl.BlockSpec((1, V7X_LANES), lambda i, j: (0, (j + 1) * per))],
        out_specs=pl.BlockSpec((tr, tn), lambda i, j: (i, j)),
        out_shape=jax.ShapeDtypeStruct((K, n_cols), BF16),
        compiler_params=_params(("parallel", "parallel"), 4 * _nbytes((tr, tn), F32), 2 * _nbytes((tr, tn), BF16)),
        name="shift_cast_cols",
    )(w, w, sc, sc)


def _load_conv_halo(buf_ref, carry_ref, first_tile):
    halo = V7X_SUBLANES

    @pl.when(first_tile)
    def _():
        buf_ref[0:halo, :] = jnp.zeros((halo, buf_ref.shape[1]), F32)

    @pl.when(jnp.logical_not(first_tile))
    def _():
        buf_ref[0:halo, :] = carry_ref[...]


def _causal_conv_rows(acc, cols, buf_ref, carry_ref, cw_ref, cb_ref, taps):
    tm = acc.shape[0]
    halo = V7X_SUBLANES
    buf_ref[halo:halo + tm, cols] = acc
    carry_ref[:, cols] = acc[tm - halo:tm, :]
    y = cb_ref[:, cols] + cw_ref[taps - 1:taps, cols] * acc
    rows = buf_ref[0:halo + tm, cols]
    for k in range(taps - 1):
        back = taps - 1 - k
        y = y + cw_ref[k:k + 1, cols] * pltpu.roll(rows, back, 0)[halo:halo + tm]
    return y


def _proj_conv_silu_kernel(h_ref, w_ref, cw_ref, cb_ref, o_ref, carry_ref, buf_ref, *,
                           tiles_per_seq, taps):
    i = pl.program_id(0)
    j = pl.program_id(1)
    _load_conv_halo(buf_ref, carry_ref.at[j], (i % tiles_per_seq) == 0)
    for cols in _col_blocks(o_ref.shape[1]):
        acc = jnp.dot(h_ref[...], w_ref[:, cols], preferred_element_type=F32)
        y = _causal_conv_rows(acc, cols, buf_ref, carry_ref.at[j], cw_ref, cb_ref, taps)
        o_ref[:, cols] = (y * jax.nn.sigmoid(y)).astype(o_ref.dtype)


def _proj_conv_silu(h, w, n_cols, cw, cb, seq_len, tm=1024, tn=1024):
    T, K = h.shape
    taps = cw.shape[0]
    nj = n_cols // tn
    kern = functools.partial(_proj_conv_silu_kernel, tiles_per_seq=seq_len // tm, taps=taps)
    return pl.pallas_call(
        kern,
        grid=(T // tm, nj),
        in_specs=[pl.BlockSpec((tm, K), lambda i, j: (i, 0)),
                  pl.BlockSpec((K, tn), lambda i, j: (0, j)),
                  pl.BlockSpec((taps, tn), lambda i, j: (0, j)),
                  pl.BlockSpec((1, tn), lambda i, j: (0, j))],
        out_specs=pl.BlockSpec((tm, tn), lambda i, j: (i, j)),
        out_shape=jax.ShapeDtypeStruct((T, n_cols), BF16),
        scratch_shapes=[pltpu.VMEM((nj, V7X_SUBLANES, tn), F32),
                        pltpu.VMEM((tm + V7X_SUBLANES, tn), F32)],
        compiler_params=_params(("arbitrary", "arbitrary"),
                                2 * _nbytes((tm, K), BF16), 2 * _nbytes((K, tn), BF16),
                                2 * _nbytes((tm, tn), BF16), 2 * _nbytes((tm, tn), F32)),
        name="proj_conv_silu",
    )(h, w, cw, cb.reshape(1, n_cols))


def _proj_act_kernel(h_ref, w_ref, b_ref, o_ref, *, act):
    for cols in _col_blocks(o_ref.shape[1]):
        acc = jnp.dot(h_ref[...], w_ref[:, cols], preferred_element_type=F32) + b_ref[:, cols]
        if act == "sigmoid":
            acc = jax.nn.sigmoid(acc)
        o_ref[:, cols] = acc.astype(o_ref.dtype)


def _proj_act(h, w, col0, n_cols, bias, act, out_dtype, tm=1024, tn=1024):
    T, K = h.shape
    tn = min(tn, n_cols)
    j0 = col0 // tn
    return pl.pallas_call(
        functools.partial(_proj_act_kernel, act=act),
        grid=(T // tm, n_cols // tn),
        in_specs=[pl.BlockSpec((tm, K), lambda i, j: (i, 0)),
                  pl.BlockSpec((K, tn), lambda i, j: (0, j0 + j)),
                  pl.BlockSpec((1, tn), lambda i, j: (0, j))],
        out_specs=pl.BlockSpec((tm, tn), lambda i, j: (i, j)),
        out_shape=jax.ShapeDtypeStruct((T, n_cols), out_dtype),
        compiler_params=_params(("parallel", "parallel"),
                                2 * _nbytes((tm, K), BF16), 2 * _nbytes((K, tn), BF16),
                                2 * _nbytes((tm, tn), out_dtype), _nbytes((tm, tn), F32)),
        name="proj_" + act,
    )(h, w, bias.reshape(1, n_cols))


def _mlstm_kernel(q_ref, k_ref, v_ref, og_ref, g_ref, gain_ref, o_ref, c_ref, n_ref, m_ref, *,
                  heads, head_dim):
    L = q_ref.shape[0]
    H, Dh = heads, head_dim
    q_scale = Dh ** -0.5

    @pl.when(pl.program_id(1) == 0)
    def _():
        c_ref[...] = jnp.zeros_like(c_ref)
        n_ref[...] = jnp.zeros_like(n_ref)
        m_ref[...] = jnp.zeros_like(m_ref)

    gates = g_ref[...]
    logf = jnp.minimum(gates, 0.0) - jnp.log1p(jnp.exp(-jnp.abs(gates)))
    row = lax.broadcasted_iota(jnp.int32, (L, L), 0)
    col = lax.broadcasted_iota(jnp.int32, (L, L), 1)
    tri = col <= row
    cum = jnp.dot(tri.astype(F32), logf, precision=lax.Precision.HIGHEST,
                  preferred_element_type=F32)
    gates_t = gates.T
    cum_t = cum.T

    for h in range(H):
        hs = slice(h * Dh, (h + 1) * Dh)
        b_col = cum[:, H + h:H + h + 1]
        u_col = gates[:, h:h + 1] - b_col
        u_row = gates_t[h:h + 1, :] - cum_t[H + h:H + h + 1, :]
        m_prev = m_ref[h, 0:1, 0:1]
        cu = jnp.max(jnp.where(tri, u_row, -jnp.inf), axis=1, keepdims=True)
        mm = jnp.maximum(m_prev, cu)
        w_intra = jnp.where(tri, jnp.exp(u_row - mm), 0.0) * q_scale
        w_inter = jnp.exp(m_prev - mm) * q_scale

        qh = q_ref[:, hs]
        kh = k_ref[:, hs]
        vh = v_ref[:, hs]
        s = lax.dot_general(qh, kh, (((1,), (1,)), ((), ())), preferred_element_type=F32)
        sqk = s * w_intra
        num = (jnp.dot(sqk.astype(BF16), vh, preferred_element_type=F32)
               + w_inter * jnp.dot(qh, c_ref[h].astype(BF16), preferred_element_type=F32))
        qn = jnp.sum(qh.astype(F32) * n_ref[h], axis=1, keepdims=True)
        den = jnp.sum(sqk, axis=1, keepdims=True) + w_inter * qn
        hh = num / jnp.maximum(jnp.abs(den), jnp.exp(-(b_col + mm)))
        y = hh * _rms_scale(hh) * gain_ref[:, hs]
        o_ref[:, hs] = (y * og_ref[:, hs].astype(F32)).astype(o_ref.dtype)

        mm_last = mm[L - 1:L, :]
        ws = jnp.exp(u_col - mm_last)
        decay = jnp.exp(m_prev - mm_last)
        kw = kh.astype(F32) * ws
        c_ref[h] = decay * c_ref[h] + lax.dot_general(
            kw.astype(BF16), vh, (((0,), (0,)), ((), ())), preferred_element_type=F32)
        n_ref[h] = decay * n_ref[h] + jnp.sum(kw, axis=0, keepdims=True)
        m_ref[h] = jnp.broadcast_to(b_col[L - 1:L, :] + mm_last, m_ref.shape[1:])


def _mlstm(qk, v, og, gates, gain, batch, seq_len):
    T = qk.shape[0]
    H, Dh, L = M_HEADS, M_HEAD_DIM, MLSTM_CHUNK
    W = H * Dh
    nt = seq_len // L
    row_blk = lambda b, t: (b * nt + t, 0)
    return pl.pallas_call(
        functools.partial(_mlstm_kernel, heads=H, head_dim=Dh),
        grid=(batch, nt),
        in_specs=[pl.BlockSpec((L, W), row_blk),
                  pl.BlockSpec((L, W), lambda b, t: (b * nt + t, 1)),
                  pl.BlockSpec((L, W), row_blk),
                  pl.BlockSpec((L, W), row_blk),
                  pl.BlockSpec((L, V7X_LANES), row_blk),
                  pl.BlockSpec((1, W), lambda b, t: (0, 0))],
        out_specs=pl.BlockSpec((L, W), row_blk),
        out_shape=jax.ShapeDtypeStruct((T, W), BF16),
        scratch_shapes=[pltpu.VMEM((H, Dh, Dh), F32),
                        pltpu.VMEM((H, 1, Dh), F32),
                        pltpu.VMEM((H, V7X_SUBLANES, V7X_LANES), F32)],
        compiler_params=_params(("arbitrary", "arbitrary"),
                                10 * _nbytes((L, W), BF16), _nbytes((H, Dh, Dh), F32)),
        name="mlstm",
    )(qk, qk, v, og, gates, gain.reshape(1, W))


def _attn_kernel(slopes_ref, lam_ref, q_ref, k_ref, v_ref, gain_ref, o_ref,
                 vt_ref, acc_ref, m_ref, s_ref, p_ref, al_ref, kmax_ref, *, tile, lam_init):
    h = pl.program_id(1)
    qi = pl.program_id(2)
    dk = A_QK_DIM
    dv = A_V_DIM
    slope = slopes_ref[h] * LOG2E
    seq = k_ref.shape[0]

    def max_row_norm_sq(x):
        sq = x.astype(F32)
        sq = sq * sq
        return jnp.maximum(jnp.max(jnp.sum(sq[:, 0:dk], axis=1, keepdims=True)),
                           jnp.max(jnp.sum(sq[:, dk:2 * dk], axis=1, keepdims=True)))

    @pl.when(qi == 0)
    def _():
        kmax = jnp.float32(0.0)
        for c in range(seq // tile):
            rows = slice(c * tile, (c + 1) * tile)
            vt_ref[0:dv, rows] = v_ref[rows, :].T
            kmax = jnp.maximum(kmax, max_row_norm_sq(k_ref[rows, :]))
        extra = lax.broadcasted_iota(jnp.int32, (ATTN_VT_PAD, seq), 0)
        vt_ref[dv:dv + ATTN_VT_PAD, :] = jnp.where(extra == 0, 1.0, 0.0).astype(BF16)
        kmax_ref[0] = kmax

    m_ref[...] = jnp.full(m_ref.shape, MASK_VALUE, F32)
    acc_ref[...] = jnp.zeros_like(acc_ref)

    krow = lax.broadcasted_iota(jnp.int32, (tile, tile), 0)
    qcol = lax.broadcasted_iota(jnp.int32, (tile, tile), 1)
    rel = (qcol - krow).astype(F32)
    bias_full = -slope * rel

    def scores(j, slot):
        kt = k_ref[pl.ds(pl.multiple_of(j * tile, tile), tile), :]
        for c in range(2):
            s_ref[slot, c] = lax.dot_general(kt[:, c * dk:(c + 1) * dk], q_ref[:, c * dk:(c + 1) * dk],
                                             (((1,), (1,)), ((), ())), preferred_element_type=F32)

    def softmax_step(slot, bias, shift):
        for c in range(2):
            sc = s_ref[slot, c] + bias
            m_prev = m_ref[c]
            m_new = jnp.maximum(m_prev, jnp.max(sc, axis=0, keepdims=True) + shift)
            m_ref[c] = m_new
            p_ref[slot, c] = jnp.exp2(sc - (m_new - shift)).astype(BF16)
            al_ref[slot, c] = jnp.exp2(m_prev - m_new)

    def accumulate(j, slot):
        vt = vt_ref[:, pl.ds(pl.multiple_of(j * tile, tile), tile)]
        for c in range(2):
            acc_ref[c] = al_ref[slot, c] * acc_ref[c] + jnp.dot(vt, p_ref[slot, c], preferred_element_type=F32)

    def tile_of(t):
        return jnp.clip(qi - 1 - t, 0, qi)

    def shift_of(t):
        return -slope * jnp.asarray((t + 1) * tile).astype(F32)

    scores(qi, 0)
    scores(tile_of(0), 1)
    allowed = (krow // CHUNK) <= (qcol // CHUNK)
    softmax_step(0, jnp.where(allowed, -slope * jnp.abs(rel), MASK_VALUE), 0.0)
    accumulate(qi, 0)
    softmax_step(1, bias_full, jnp.where(qi > 0, shift_of(0), MASK_VALUE))
    accumulate(tile_of(0), 1)
    p_ref[1] = jnp.zeros(p_ref.shape[1:], BF16)
    al_ref[1] = jnp.ones(al_ref.shape[1:], F32)

    qk_bound = jnp.sqrt(max_row_norm_sq(q_ref[...]) * kmax_ref[0]) + ATTN_BOUND_MARGIN
    reach = (qk_bound + UNDERFLOW_EXP2 - jnp.min(m_ref[...])) / (slope * tile)
    n_keep = jnp.minimum(qi, jnp.clip(reach, 0.0, 1e6).astype(jnp.int32) + 1)
    rest = jnp.maximum(n_keep - 1, 0)
    first = lax.rem(rest, 2)
    pairs = lax.div(rest, 2)

    @pl.when(first == 1)
    def _():
        scores(tile_of(1), 0)
        softmax_step(0, bias_full, shift_of(1))
        accumulate(tile_of(1), 0)

    @pl.when(pairs > 0)
    def _():
        scores(tile_of(1 + first), 0)

    def pair(g, carry):
        a = 1 + first + 2 * g
        accumulate(tile_of(a - 1), 1)
        softmax_step(0, bias_full, shift_of(a))
        scores(tile_of(a + 1), 1)
        accumulate(tile_of(a), 0)
        softmax_step(1, bias_full, shift_of(a + 1))
        scores(tile_of(a + 2), 0)
        return carry

    lax.fori_loop(0, pairs, pair, 0)
    accumulate(tile_of(n_keep - 1), 1)

    lam_vec = lam_ref[...]
    lam = (jnp.exp(jnp.sum(lam_vec[0:1] * lam_vec[1:2], axis=1, keepdims=True))
           - jnp.exp(jnp.sum(lam_vec[2:3] * lam_vec[3:4], axis=1, keepdims=True)) + lam_init)
    o = (acc_ref[0, 0:dv] / acc_ref[0, dv:dv + 1]
         - lam * (acc_ref[1, 0:dv] / acc_ref[1, dv:dv + 1]))
    scale = lax.rsqrt(jnp.mean(o * o, axis=0, keepdims=True) + EPS) * (1.0 - lam_init)
    o_ref[...] = ((o * scale).T * gain_ref[...]).astype(o_ref.dtype)


def _diff_attention(a, lam_vecs, gain, lam_init, batch, seq_len):
    T = a.shape[0]
    H, Dv = A_HEADS, A_V_DIM
    W = H * Dv
    tile = ATTN_TILE
    assert tile % CHUNK == 0 and seq_len % tile == 0
    nq = seq_len // tile
    slopes = jnp.asarray(ALIBI_SLOPES, F32)
    return pl.pallas_call(
        functools.partial(_attn_kernel, tile=tile, lam_init=lam_init),
        grid=(batch, H, nq),
        in_specs=[pl.BlockSpec(memory_space=pltpu.SMEM),
                  pl.BlockSpec((4, A_QK_DIM), lambda b, h, i: (0, 0)),
                  pl.BlockSpec((tile, Dv), lambda b, h, i: (b * nq + i, h)),
                  pl.BlockSpec((seq_len, Dv), lambda b, h, i: (b, H + h)),
                  pl.BlockSpec((seq_len, Dv), lambda b, h, i: (b, 2 * H + h)),
                  pl.BlockSpec((1, Dv), lambda b, h, i: (0, h))],
        out_specs=pl.BlockSpec((tile, Dv), lambda b, h, i: (b * nq + i, h)),
        out_shape=jax.ShapeDtypeStruct((T, W), BF16),
        scratch_shapes=[pltpu.VMEM((Dv + ATTN_VT_PAD, seq_len), BF16),
                        pltpu.VMEM((2, Dv + ATTN_VT_PAD, tile), F32),
                        pltpu.VMEM((2, 1, tile), F32),
                        pltpu.VMEM((2, 2, tile, tile), F32),
                        pltpu.VMEM((2, 2, tile, tile), BF16),
                        pltpu.VMEM((2, 2, 1, tile), F32),
                        pltpu.SMEM((1,), F32)],
        compiler_params=_params(("arbitrary", "arbitrary", "arbitrary"),
                                5 * _nbytes((seq_len, Dv), BF16), 4 * _nbytes((tile, Dv), BF16),
                                2 * _nbytes((Dv, tile), F32), 4 * _nbytes((tile, tile), F32),
                                4 * _nbytes((tile, tile), BF16)),
        name="diff_attention",
    )(slopes, lam_vecs, a, a, a, gain.reshape(1, W))


def _merge_kernel(am_ref, aa_ref, wm_ref, wa_ref, gm_ref, ga_ref, o_ref):
    for cols in _col_blocks(o_ref.shape[1]):
        ym = jnp.dot(am_ref[...], wm_ref[:, cols], preferred_element_type=F32)
        ya = jnp.dot(aa_ref[...], wa_ref[:, cols], preferred_element_type=F32)
        o_ref[:, cols] = (gm_ref[:, cols].astype(F32) * ym + ga_ref[:, cols].astype(F32) * ya).astype(o_ref.dtype)


def _branch_merge(hm, ha, wm, wa, g, tm=1024, tn=1024):
    T, K = hm.shape
    N = wm.shape[1]
    nj = N // tn
    return pl.pallas_call(
        _merge_kernel,
        grid=(T // tm, nj),
        in_specs=[pl.BlockSpec((tm, K), lambda i, j: (i, 0)),
                  pl.BlockSpec((tm, K), lambda i, j: (i, 0)),
                  pl.BlockSpec((K, tn), lambda i, j: (0, j)),
                  pl.BlockSpec((K, tn), lambda i, j: (0, j)),
                  pl.BlockSpec((tm, tn), lambda i, j: (i, j)),
                  pl.BlockSpec((tm, tn), lambda i, j: (i, nj + j))],
        out_specs=pl.BlockSpec((tm, tn), lambda i, j: (i, j)),
        out_shape=jax.ShapeDtypeStruct((T, N), BF16),
        compiler_params=_params(("parallel", "parallel"),
                                4 * _nbytes((tm, K), BF16), 4 * _nbytes((K, tn), BF16),
                                6 * _nbytes((tm, tn), BF16), 2 * _nbytes((tm, tn), F32)),
        name="branch_merge",
    )(hm, ha, wm, wa, g, g)


def _outproj_kernel(mix_ref, w_ref, x_ref, gpost_ref, gpre_ref, x1_ref, h2_ref):
    y = jnp.dot(mix_ref[...], w_ref[...], preferred_element_type=F32)
    x1 = x_ref[...] + y * _rms_scale(y) * gpost_ref[...]
    x1_ref[...] = x1
    h2_ref[...] = (x1 * _rms_scale(x1) * gpre_ref[...]).astype(h2_ref.dtype)


def _outproj(mix, w, x, g_post, g_pre, tm=512):
    T, K = mix.shape
    D = w.shape[1]
    row = lambda i: (i, 0)
    fixed = lambda i: (0, 0)
    return pl.pallas_call(
        _outproj_kernel,
        grid=(T // tm,),
        in_specs=[pl.BlockSpec((tm, K), row),
                  pl.BlockSpec((K, D), fixed),
                  pl.BlockSpec((tm, D), row),
                  pl.BlockSpec((1, D), fixed),
                  pl.BlockSpec((1, D), fixed)],
        out_specs=[pl.BlockSpec((tm, D), row), pl.BlockSpec((tm, D), row)],
        out_shape=[jax.ShapeDtypeStruct((T, D), F32), jax.ShapeDtypeStruct((T, D), BF16)],
        compiler_params=_params(("parallel",),
                                2 * _nbytes((tm, K), BF16), 2 * _nbytes((K, D), BF16),
                                4 * _nbytes((tm, D), F32), 2 * _nbytes((tm, D), BF16),
                                2 * _nbytes((tm, D), F32)),
        name="outproj_norm",
    )(mix, w, x, g_post.reshape(1, D), g_pre.reshape(1, D))


def _gelu_tanh(x):
    return 0.5 * x * (1.0 + jnp.tanh(math.sqrt(2.0 / math.pi) * (x + 0.044715 * (x * x * x))))


def _ffn_up_kernel(h_ref, wg_ref, wv_ref, cwg_ref, cwv_ref, cbg_ref, cbv_ref, o_ref,
                   carry_ref, buf_ref, *, tiles_per_seq, taps):
    i = pl.program_id(0)
    j = pl.program_id(1)
    first = (i % tiles_per_seq) == 0
    _load_conv_halo(buf_ref.at[0], carry_ref.at[j, 0], first)
    _load_conv_halo(buf_ref.at[1], carry_ref.at[j, 1], first)
    for cols in _col_blocks(o_ref.shape[1]):
        gate = _causal_conv_rows(jnp.dot(h_ref[...], wg_ref[:, cols], preferred_element_type=F32), cols,
                                 buf_ref.at[0], carry_ref.at[j, 0], cwg_ref, cbg_ref, taps)
        val = _causal_conv_rows(jnp.dot(h_ref[...], wv_ref[:, cols], preferred_element_type=F32), cols,
                                buf_ref.at[1], carry_ref.at[j, 1], cwv_ref, cbv_ref, taps)
        o_ref[:, cols] = (_gelu_tanh(gate) * val).astype(o_ref.dtype)


def _ffn_up(h2, w_up, cw, cb, seq_len, tm=1024, tf=512):
    T, K = h2.shape
    F = w_up.shape[1] // 2
    taps = cw.shape[0]
    nj = F // tf
    gate_col = lambda i, j: (0, j)
    val_col = lambda i, j: (0, nj + j)
    cb2 = cb.reshape(1, 2 * F)
    return pl.pallas_call(
        functools.partial(_ffn_up_kernel, tiles_per_seq=seq_len // tm, taps=taps),
        grid=(T // tm, nj),
        in_specs=[pl.BlockSpec((tm, K), lambda i, j: (i, 0)),
                  pl.BlockSpec((K, tf), gate_col),
                  pl.BlockSpec((K, tf), val_col),
                  pl.BlockSpec((taps, tf), gate_col),
                  pl.BlockSpec((taps, tf), val_col),
                  pl.BlockSpec((1, tf), gate_col),
                  pl.BlockSpec((1, tf), val_col)],
        out_specs=pl.BlockSpec((tm, tf), lambda i, j: (i, j)),
        out_shape=jax.ShapeDtypeStruct((T, F), BF16),
        scratch_shapes=[pltpu.VMEM((nj, 2, V7X_SUBLANES, tf), F32),
                        pltpu.VMEM((2, tm + V7X_SUBLANES, tf), F32)],
        compiler_params=_params(("arbitrary", "arbitrary"),
                                2 * _nbytes((tm, K), BF16), 4 * _nbytes((K, tf), BF16),
                                2 * _nbytes((tm, tf), BF16), 4 * _nbytes((tm, tf), F32)),
        name="ffn_up_conv_gelu",
    )(h2, w_up, w_up, cw, cw, cb2, cb2)


def _ffn_down_kernel(a_ref, w_ref, x1_ref, g_ref, o_ref, acc_ref):
    k = pl.program_id(1)

    @pl.when(k == 0)
    def _():
        acc_ref[...] = jnp.zeros_like(acc_ref)

    for cols in _col_blocks(acc_ref.shape[1]):
        acc_ref[:, cols] += jnp.dot(a_ref[...], w_ref[:, cols], preferred_element_type=F32)

    @pl.when(k == pl.num_programs(1) - 1)
    def _():
        y = acc_ref[...]
        o_ref[...] = x1_ref[...] + y * _rms_scale(y) * g_ref[...]


def _ffn_down(act, w, x1, g, tm=512, tk=2816):
    T, F = act.shape
    assert F % tk == 0 and tk % V7X_MXU_WIDTH == 0
    D = w.shape[1]
    return pl.pallas_call(
        _ffn_down_kernel,
        grid=(T // tm, F // tk),
        in_specs=[pl.BlockSpec((tm, tk), lambda i, k: (i, k)),
                  pl.BlockSpec((tk, D), lambda i, k: (k, 0)),
                  pl.BlockSpec((tm, D), lambda i, k: (i, 0)),
                  pl.BlockSpec((1, D), lambda i, k: (0, 0))],
        out_specs=pl.BlockSpec((tm, D), lambda i, k: (i, 0)),
        out_shape=jax.ShapeDtypeStruct((T, D), F32),
        scratch_shapes=[pltpu.VMEM((tm, D), F32)],
        compiler_params=_params(("parallel", "arbitrary"),
                                2 * _nbytes((tm, tk), BF16), 2 * _nbytes((tk, D), BF16),
                                5 * _nbytes((tm, D), F32)),
        name="ffn_down_norm",
    )(act, w, x1, g.reshape(1, D))


def _layer(x, l, g_pre_mix, w_in, b_gates, m_conv_w, m_conv_b, m_head_norm,
           lambda_q1, lambda_k1, lambda_q2, lambda_k2, a_head_norm,
           w_branch_m, w_branch_a, w_out, g_post_mix, g_pre_ffn,
           w_up, ffn_conv_w, ffn_conv_b, w_down, g_post_ffn, batch, seq_len):
    D = x.shape[1]
    mw = M_HEADS * M_HEAD_DIM
    aw = A_HEADS * A_V_DIM
    n_gate = 2 * M_HEADS
    pre = 4 * mw
    post0 = pre + n_gate
    lam_init = 0.8 - 0.6 * math.exp(-0.3 * l)

    w_pre = _cast_cols(w_in, pre)
    post_scale = jnp.concatenate([jnp.ones((n_gate,), F32),
                                  jnp.full((aw,), A_QK_DIM ** -0.5 * LOG2E, F32),
                                  jnp.ones((2 * aw + 2 * D,), F32)])
    w_post = _shift_cast_cols(w_in, pre, n_gate, 3 * aw + 2 * D, post_scale)
    w_gate = jnp.pad(w_in[:, pre:post0], ((0, 0), (0, V7X_LANES - n_gate))).astype(BF16)
    b_gate = jnp.pad(b_gates.astype(F32), (0, V7X_LANES - n_gate))
    lam_vecs = jnp.stack([lambda_q1, lambda_k1, lambda_q2, lambda_k2]).astype(F32)
    no_bias = lambda n: jnp.zeros((n,), F32)

    h = _rmsnorm(x, g_pre_mix)
    qk = _proj_conv_silu(h, w_pre, 2 * mw, m_conv_w, m_conv_b, seq_len)
    v = _proj_act(h, w_pre, 2 * mw, mw, no_bias(mw), "identity", BF16)
    og = _proj_act(h, w_pre, 3 * mw, mw, no_bias(mw), "sigmoid", BF16)
    gates = _proj_act(h, w_gate, 0, V7X_LANES, b_gate, "identity", F32)
    a = _proj_act(h, w_post, 0, 3 * aw, no_bias(3 * aw), "identity", BF16)
    g = _proj_act(h, w_post, 3 * aw, 2 * D, no_bias(2 * D), "sigmoid", BF16)

    hm = _mlstm(qk, v, og, gates, m_head_norm, batch, seq_len)
    ha = _diff_attention(a, lam_vecs, a_head_norm, lam_init, batch, seq_len)

    mix = _branch_merge(hm, ha, w_branch_m.astype(BF16), w_branch_a.astype(BF16), g)
    x1, h2 = _outproj(mix, w_out.astype(BF16), x, g_post_mix, g_pre_ffn)
    act = _ffn_up(h2, w_up.astype(BF16), ffn_conv_w, ffn_conv_b, seq_len)
    return _ffn_down(act, w_down.astype(BF16), x1, g_post_ffn)


def kernel(x, g_pre_mix, w_in, b_gates, m_conv_w, m_conv_b, m_head_norm, lambda_q1, lambda_k1, lambda_q2, lambda_k2, a_head_norm, w_branch_m, w_branch_a, w_out, g_post_mix, g_pre_ffn, w_up, ffn_conv_w, ffn_conv_b, w_down, g_post_ffn):
    B, S, D = x.shape
    layers = (g_pre_mix, w_in, b_gates, m_conv_w, m_conv_b, m_head_norm, lambda_q1, lambda_k1,
              lambda_q2, lambda_k2, a_head_norm, w_branch_m, w_branch_a, w_out, g_post_mix,
              g_pre_ffn, w_up, ffn_conv_w, ffn_conv_b, w_down, g_post_ffn)
    y = x.reshape(B * S, D)
    for l in range(w_in.shape[0]):
        y = _layer(y, l, *[p[l] for p in layers], batch=B, seq_len=S)
    return y.reshape(B, S, D)
```

```python
import functools
import math

import numpy as np
import jax
import jax.numpy as jnp
from jax import lax
from jax.experimental import pallas as pl
from jax.experimental.pallas import tpu as pltpu

F32 = jnp.float32
BF16 = jnp.bfloat16

EPS = 1e-6
CHUNK = 64
M_HEADS = 8
M_HEAD_DIM = 256
A_HEADS = 8
A_QK_DIM = 128
A_V_DIM = 256
ALIBI_SLOPES = 2.0 ** (-8.0 * np.arange(1, A_HEADS + 1) / A_HEADS)

V7X_LANES = 128
V7X_SUBLANES = 8
V7X_MXU_WIDTH = 256
V7X_NUM_MXU = 2
V7X_SCOPED_VMEM_BYTES = 60000 * 1024
COMPILER_TEMP_BYTES = 16 * 1024 * 1024

MLSTM_CHUNK = V7X_MXU_WIDTH
ATTN_TILE = 512
ATTN_VT_PAD = 16
UNDERFLOW_EXP2 = 160.0
ATTN_BOUND_MARGIN = 1.0
MASK_VALUE = -1e30
LOG2E = math.log2(math.e)


def _vmem_limit(*buffer_bytes):
    return int(min(sum(buffer_bytes) + COMPILER_TEMP_BYTES, V7X_SCOPED_VMEM_BYTES))


def _nbytes(shape, dtype):
    return int(np.prod(shape)) * jnp.dtype(dtype).itemsize


def _params(semantics, *buffer_bytes):
    return pltpu.CompilerParams(dimension_semantics=semantics,
                                vmem_limit_bytes=_vmem_limit(*buffer_bytes))


def _rms_scale(y):
    return lax.rsqrt(jnp.mean(y * y, axis=-1, keepdims=True) + EPS)


def _col_blocks(width):
    sub = V7X_NUM_MXU * V7X_MXU_WIDTH
    if width % sub:
        return [slice(0, width)]
    return [slice(c, c + sub) for c in range(0, width, sub)]


def _rmsnorm_kernel(x_ref, g_ref, o_ref):
    x = x_ref[...]
    o_ref[...] = (x * _rms_scale(x) * g_ref[...]).astype(o_ref.dtype)


def _rmsnorm(x, g, tm=512):
    T, D = x.shape
    return pl.pallas_call(
        _rmsnorm_kernel,
        grid=(T // tm,),
        in_specs=[pl.BlockSpec((tm, D), lambda i: (i, 0)),
                  pl.BlockSpec((1, D), lambda i: (0, 0))],
        out_specs=pl.BlockSpec((tm, D), lambda i: (i, 0)),
        out_shape=jax.ShapeDtypeStruct((T, D), BF16),
        compiler_params=_params(("parallel",), 2 * _nbytes((tm, D), F32), 2 * _nbytes((tm, D), BF16)),
        name="rmsnorm",
    )(x, g.reshape(1, D))


def _cast_rows_kernel(main_ref, next_ref, o_ref, *, lead):
    if lead:
        rows = jnp.concatenate([main_ref[lead:, :], next_ref[...]], axis=0)
    else:
        rows = main_ref[...]
    o_ref[...] = rows.astype(o_ref.dtype)


def _cast_rows(wt, row0, lead, n_rows, tr=512):
    K = wt.shape[1]
    nxt = max(lead, V7X_SUBLANES)
    assert row0 % tr == 0 and n_rows % tr == 0 and lead % V7X_SUBLANES == 0 and tr % nxt == 0
    i0 = row0 // tr
    return pl.pallas_call(
        functools.partial(_cast_rows_kernel, lead=lead),
        grid=(n_rows // tr,),
        in_specs=[pl.BlockSpec((tr, K), lambda i: (i0 + i, 0)),
                  pl.BlockSpec((nxt, K), lambda i: ((i0 + i + 1) * (tr // nxt), 0))],
        out_specs=pl.BlockSpec((tr, K), lambda i: (i, 0)),
        out_shape=jax.ShapeDtypeStruct((n_rows, K), BF16),
        compiler_params=_params(("parallel",), 2 * _nbytes((tr, K), F32), 2 * _nbytes((tr, K), BF16)),
        name="cast_rows",
    )(wt, wt)


def _load_conv_halo(buf_ref, carry_ref, first_tile):
    halo = V7X_SUBLANES

    @pl.when(first_tile)
    def _():
        buf_ref[0:halo, :] = jnp.zeros((halo, buf_ref.shape[1]), F32)

    @pl.when(jnp.logical_not(first_tile))
    def _():
        buf_ref[0:halo, :] = carry_ref[...]


def _causal_conv_rows(acc, rows, cols, buf_ref, carry_ref, cw_ref, cb_ref, taps):
    halo = V7X_SUBLANES
    r0, r1 = rows.start, rows.stop
    buf_ref[halo + r0:halo + r1, cols] = acc
    if r1 == buf_ref.shape[0] - halo:
        carry_ref[:, cols] = acc[r1 - r0 - halo:, :]
    y = cb_ref[:, cols] + cw_ref[taps - 1:taps, cols] * acc
    window = buf_ref[r0:halo + r1, cols]
    for k in range(taps - 1):
        back = taps - 1 - k
        y = y + cw_ref[k:k + 1, cols] * pltpu.roll(window, back, 0)[halo:]
    return y


def _row_blocks(tm, split):
    step = tm // split
    return [slice(r, r + step) for r in range(0, tm, step)]


def _dot_nt(x, wt):
    return lax.dot_general(x, wt, (((1,), (1,)), ((), ())), preferred_element_type=F32)


def _proj_conv_silu_kernel(h_ref, wt_ref, cw_ref, cb_ref, o_ref, carry_ref, buf_ref, *,
                           tiles_per_seq, taps, row_split):
    i = pl.program_id(0)
    j = pl.program_id(1)
    _load_conv_halo(buf_ref, carry_ref.at[j], (i % tiles_per_seq) == 0)
    for rows in _row_blocks(o_ref.shape[0], row_split):
        for cols in _col_blocks(o_ref.shape[1]):
            acc = _dot_nt(h_ref[rows, :], wt_ref[cols, :])
            y = _causal_conv_rows(acc, rows, cols, buf_ref, carry_ref.at[j], cw_ref, cb_ref, taps)
            o_ref[rows, cols] = (y * jax.nn.sigmoid(y)).astype(o_ref.dtype)


def _proj_conv_silu(h, wt, n_cols, cw, cb, seq_len, tm=1024, tn=1024, row_split=1):
    T, K = h.shape
    taps = cw.shape[0]
    nj = n_cols // tn
    kern = functools.partial(_proj_conv_silu_kernel, tiles_per_seq=seq_len // tm, taps=taps,
                             row_split=row_split)
    return pl.pallas_call(
        kern,
        grid=(T // tm, nj),
        in_specs=[pl.BlockSpec((tm, K), lambda i, j: (i, 0)),
                  pl.BlockSpec((tn, K), lambda i, j: (j, 0)),
                  pl.BlockSpec((taps, tn), lambda i, j: (0, j)),
                  pl.BlockSpec((1, tn), lambda i, j: (0, j))],
        out_specs=pl.BlockSpec((tm, tn), lambda i, j: (i, j)),
        out_shape=jax.ShapeDtypeStruct((T, n_cols), BF16),
        scratch_shapes=[pltpu.VMEM((nj, V7X_SUBLANES, tn), F32),
                        pltpu.VMEM((tm + V7X_SUBLANES, tn), F32)],
        compiler_params=_params(("arbitrary", "arbitrary"),
                                2 * _nbytes((tm, K), BF16), 2 * _nbytes((K, tn), BF16),
                                2 * _nbytes((tm, tn), BF16), 2 * _nbytes((tm, tn), F32)),
        name="proj_conv_silu",
    )(h, wt, cw, cb.reshape(1, n_cols))


def _proj_act_kernel(h_ref, wt_ref, sc_ref, b_ref, o_ref, *, act):
    for cols in _col_blocks(o_ref.shape[1]):
        acc = _dot_nt(h_ref[...], wt_ref[cols, :]) * sc_ref[:, cols] + b_ref[:, cols]
        if act == "sigmoid":
            acc = jax.nn.sigmoid(acc)
        o_ref[:, cols] = acc.astype(o_ref.dtype)


def _proj_act(h, wt, row0, n_cols, scale, bias, act, out_dtype, tm=1024, tn=1024):
    T, K = h.shape
    tn = min(tn, n_cols)
    j0 = row0 // tn
    return pl.pallas_call(
        functools.partial(_proj_act_kernel, act=act),
        grid=(T // tm, n_cols // tn),
        in_specs=[pl.BlockSpec((tm, K), lambda i, j: (i, 0)),
                  pl.BlockSpec((tn, K), lambda i, j: (j0 + j, 0)),
                  pl.BlockSpec((1, tn), lambda i, j: (0, j)),
                  pl.BlockSpec((1, tn), lambda i, j: (0, j))],
        out_specs=pl.BlockSpec((tm, tn), lambda i, j: (i, j)),
        out_shape=jax.ShapeDtypeStruct((T, n_cols), out_dtype),
        compiler_params=_params(("parallel", "parallel"),
                                2 * _nbytes((tm, K), BF16), 2 * _nbytes((K, tn), BF16),
                                2 * _nbytes((tm, tn), out_dtype), _nbytes((tm, tn), F32)),
        name="proj_" + act,
    )(h, wt, scale.reshape(1, n_cols), bias.reshape(1, n_cols))


def _mlstm_kernel(q_ref, k_ref, v_ref, og_ref, g_ref, gain_ref, o_ref, c_ref, n_ref, m_ref, *,
                  heads, head_dim):
    L = q_ref.shape[0]
    H, Dh = heads, head_dim
    q_scale = Dh ** -0.5

    @pl.when(pl.program_id(1) == 0)
    def _():
        c_ref[...] = jnp.zeros_like(c_ref)
        n_ref[...] = jnp.zeros_like(n_ref)
        m_ref[...] = jnp.zeros_like(m_ref)

    gates = g_ref[...]
    logf = jnp.minimum(gates, 0.0) - jnp.log1p(jnp.exp(-jnp.abs(gates)))
    srow = lax.broadcasted_iota(jnp.int32, (L, L), 0)
    tcol = lax.broadcasted_iota(jnp.int32, (L, L), 1)
    causal = srow <= tcol
    cum = jnp.dot((tcol <= srow).astype(F32), logf, precision=lax.Precision.HIGHEST,
                  preferred_element_type=F32)
    u_all = gates - pltpu.roll(cum, V7X_LANES - H, 1)
    u_all_t = u_all.T
    cum_t = cum.T

    for h in range(H):
        hs = slice(h * Dh, (h + 1) * Dh)
        u_row = u_all_t[h:h + 1, :]
        b_row = cum_t[H + h:H + h + 1, :]
        u_keys = jnp.broadcast_to(u_all[:, h:h + 1], (L, L))
        m_prev = m_ref[h, 0:1, 0:1]
        cu = jnp.max(jnp.where(causal, u_keys, -jnp.inf), axis=0, keepdims=True)
        mm = jnp.maximum(m_prev, cu)
        w_intra = jnp.where(causal, jnp.exp(u_keys - mm), 0.0) * q_scale
        w_inter = jnp.exp(m_prev - mm) * q_scale

        qh = q_ref[:, hs]
        kh = k_ref[:, hs]
        vt = v_ref[:, hs].T
        sqk = _dot_nt(kh, qh) * w_intra
        num = (jnp.dot(vt, sqk.astype(BF16), preferred_element_type=F32)
               + w_inter * _dot_nt(c_ref[h].astype(BF16), qh))
        qn = _dot_nt(n_ref[h].astype(BF16), qh)[0:1, :]
        den = jnp.sum(sqk, axis=0, keepdims=True) + w_inter * qn
        hh = num / jnp.maximum(jnp.abs(den), jnp.exp(-(b_row + mm)))
        scale = lax.rsqrt(jnp.mean(hh * hh, axis=0, keepdims=True) + EPS)
        y = (hh * scale).T * gain_ref[:, hs]
        o_ref[:, hs] = (y * og_ref[:, hs].astype(F32)).astype(o_ref.dtype)

        mm_last = mm[:, L - 1:L]
        ws = jnp.exp(u_row - mm_last)
        decay = jnp.exp(m_prev - mm_last)
        c_ref[h] = decay * c_ref[h] + jnp.dot((vt.astype(F32) * ws).astype(BF16), kh,
                                              preferred_element_type=F32)
        ws_rows = jnp.broadcast_to(ws, (V7X_SUBLANES, L)).astype(BF16)
        n_ref[h] = decay * n_ref[h] + jnp.dot(ws_rows, kh, preferred_element_type=F32)
        m_ref[h] = jnp.broadcast_to(b_row[:, L - 1:L] + mm_last, m_ref.shape[1:])


def _mlstm(qk, v, og, gates, gain, batch, seq_len):
    T = qk.shape[0]
    H, Dh, L = M_HEADS, M_HEAD_DIM, MLSTM_CHUNK
    W = H * Dh
    nt = seq_len // L
    row_blk = lambda b, t: (b * nt + t, 0)
    return pl.pallas_call(
        functools.partial(_mlstm_kernel, heads=H, head_dim=Dh),
        grid=(batch, nt),
        in_specs=[pl.BlockSpec((L, W), row_blk),
                  pl.BlockSpec((L, W), lambda b, t: (b * nt + t, 1)),
                  pl.BlockSpec((L, W), row_blk),
                  pl.BlockSpec((L, W), row_blk),
                  pl.BlockSpec((L, V7X_LANES), row_blk),
                  pl.BlockSpec((1, W), lambda b, t: (0, 0))],
        out_specs=pl.BlockSpec((L, W), row_blk),
        out_shape=jax.ShapeDtypeStruct((T, W), BF16),
        scratch_shapes=[pltpu.VMEM((H, Dh, Dh), F32),
                        pltpu.VMEM((H, V7X_SUBLANES, Dh), F32),
                        pltpu.VMEM((H, V7X_SUBLANES, V7X_LANES), F32)],
        compiler_params=_params(("arbitrary", "arbitrary"),
                                10 * _nbytes((L, W), BF16), _nbytes((H, Dh, Dh), F32)),
        name="mlstm",
    )(qk, qk, v, og, gates, gain.reshape(1, W))


def _attn_kernel(slopes_ref, lam_ref, q_ref, k_ref, v_ref, gain_ref, o_ref,
                 vt_ref, acc_ref, m_ref, s_ref, p_ref, al_ref, kmax_ref, *, tile, lam_init):
    h = pl.program_id(1)
    qi = pl.program_id(2)
    dk = A_QK_DIM
    dv = A_V_DIM
    slope = slopes_ref[h] * LOG2E
    seq = k_ref.shape[0]

    def max_row_norm_sq(x):
        sq = x.astype(F32)
        sq = sq * sq
        return jnp.maximum(jnp.max(jnp.sum(sq[:, 0:dk], axis=1, keepdims=True)),
                           jnp.max(jnp.sum(sq[:, dk:2 * dk], axis=1, keepdims=True)))

    @pl.when(qi == 0)
    def _():
        kmax = jnp.float32(0.0)
        for c in range(seq // tile):
            rows = slice(c * tile, (c + 1) * tile)
            vt_ref[0:dv, rows] = v_ref[rows, :].T
            kmax = jnp.maximum(kmax, max_row_norm_sq(k_ref[rows, :]))
        extra = lax.broadcasted_iota(jnp.int32, (ATTN_VT_PAD, seq), 0)
        vt_ref[dv:dv + ATTN_VT_PAD, :] = jnp.where(extra == 0, 1.0, 0.0).astype(BF16)
        kmax_ref[0] = kmax

    m_ref[...] = jnp.full(m_ref.shape, MASK_VALUE, F32)
    acc_ref[...] = jnp.zeros_like(acc_ref)

    krow = lax.broadcasted_iota(jnp.int32, (tile, tile), 0)
    qcol = lax.broadcasted_iota(jnp.int32, (tile, tile), 1)
    rel = (qcol - krow).astype(F32)
    bias_full = -slope * rel

    def scores(j, slot):
        kt = k_ref[pl.ds(pl.multiple_of(j * tile, tile), tile), :]
        for c in range(2):
            s_ref[slot, c] = lax.dot_general(kt[:, c * dk:(c + 1) * dk], q_ref[:, c * dk:(c + 1) * dk],
                                             (((1,), (1,)), ((), ())), preferred_element_type=F32)

    def softmax_step(slot, bias, shift):
        for c in range(2):
            sc = s_ref[slot, c] + bias
            m_prev = m_ref[c]
            m_new = jnp.maximum(m_prev, jnp.max(sc, axis=0, keepdims=True) + shift)
            m_ref[c] = m_new
            p_ref[slot, c] = jnp.exp2(sc - (m_new - shift)).astype(BF16)
            al_ref[slot, c] = jnp.exp2(m_prev - m_new)

    def accumulate(j, slot):
        vt = vt_ref[:, pl.ds(pl.multiple_of(j * tile, tile), tile)]
        for c in range(2):
            acc_ref[c] = al_ref[slot, c] * acc_ref[c] + jnp.dot(vt, p_ref[slot, c], preferred_element_type=F32)

    def tile_of(t):
        return jnp.clip(qi - 1 - t, 0, qi)

    def shift_of(t):
        return -slope * jnp.asarray((t + 1) * tile).astype(F32)

    scores(qi, 0)
    scores(tile_of(0), 1)
    allowed = (krow // CHUNK) <= (qcol // CHUNK)
    softmax_step(0, jnp.where(allowed, -slope * jnp.abs(rel), MASK_VALUE), 0.0)
    accumulate(qi, 0)
    softmax_step(1, bias_full, jnp.where(qi > 0, shift_of(0), MASK_VALUE))
    accumulate(tile_of(0), 1)
    p_ref[1] = jnp.zeros(p_ref.shape[1:], BF16)
    al_ref[1] = jnp.ones(al_ref.shape[1:], F32)

    qk_bound = jnp.sqrt(max_row_norm_sq(q_ref[...]) * kmax_ref[0]) + ATTN_BOUND_MARGIN
    reach = (qk_bound + UNDERFLOW_EXP2 - jnp.min(m_ref[...])) / (slope * tile)
    n_keep = jnp.minimum(qi, jnp.clip(reach, 0.0, 1e6).astype(jnp.int32) + 1)
    rest = jnp.maximum(n_keep - 1, 0)
    first = lax.rem(rest, 2)
    pairs = lax.div(rest, 2)

    @pl.when(first == 1)
    def _():
        scores(tile_of(1), 0)
        softmax_step(0, bias_full, shift_of(1))
        accumulate(tile_of(1), 0)

    @pl.when(pairs > 0)
    def _():
        scores(tile_of(1 + first), 0)

    def pair(g, carry):
        a = 1 + first + 2 * g
        accumulate(tile_of(a - 1), 1)
        softmax_step(0, bias_full, shift_of(a))
        scores(tile_of(a + 1), 1)
        accumulate(tile_of(a), 0)
        softmax_step(1, bias_full, shift_of(a + 1))
        scores(tile_of(a + 2), 0)
        return carry

    lax.fori_loop(0, pairs, pair, 0)

    @pl.when(pairs > 0)
    def _():
        accumulate(tile_of(n_keep - 1), 1)

    lam_vec = lam_ref[...]
    lam = (jnp.exp(jnp.sum(lam_vec[0:1] * lam_vec[1:2], axis=1, keepdims=True))
           - jnp.exp(jnp.sum(lam_vec[2:3] * lam_vec[3:4], axis=1, keepdims=True)) + lam_init)
    o = (acc_ref[0, 0:dv] / acc_ref[0, dv:dv + 1]
         - lam * (acc_ref[1, 0:dv] / acc_ref[1, dv:dv + 1]))
    scale = lax.rsqrt(jnp.mean(o * o, axis=0, keepdims=True) + EPS) * (1.0 - lam_init)
    o_ref[...] = ((o * scale).T * gain_ref[...]).astype(o_ref.dtype)


def _diff_attention(a, lam_vecs, gain, lam_init, batch, seq_len):
    T = a.shape[0]
    H, Dv = A_HEADS, A_V_DIM
    W = H * Dv
    tile = ATTN_TILE
    assert tile % CHUNK == 0 and seq_len % tile == 0
    nq = seq_len // tile
    slopes = jnp.asarray(ALIBI_SLOPES, F32)
    return pl.pallas_call(
        functools.partial(_attn_kernel, tile=tile, lam_init=lam_init),
        grid=(batch, H, nq),
        in_specs=[pl.BlockSpec(memory_space=pltpu.SMEM),
                  pl.BlockSpec((4, A_QK_DIM), lambda b, h, i: (0, 0)),
                  pl.BlockSpec((tile, Dv), lambda b, h, i: (b * nq + i, h)),
                  pl.BlockSpec((seq_len, Dv), lambda b, h, i: (b, H + h)),
                  pl.BlockSpec((seq_len, Dv), lambda b, h, i: (b, 2 * H + h)),
                  pl.BlockSpec((1, Dv), lambda b, h, i: (0, h))],
        out_specs=pl.BlockSpec((tile, Dv), lambda b, h, i: (b * nq + i, h)),
        out_shape=jax.ShapeDtypeStruct((T, W), BF16),
        scratch_shapes=[pltpu.VMEM((Dv + ATTN_VT_PAD, seq_len), BF16),
                        pltpu.VMEM((2, Dv + ATTN_VT_PAD, tile), F32),
                        pltpu.VMEM((2, 1, tile), F32),
                        pltpu.VMEM((2, 2, tile, tile), F32),
                        pltpu.VMEM((2, 2, tile, tile), BF16),
                        pltpu.VMEM((2, 2, 1, tile), F32),
                        pltpu.SMEM((1,), F32)],
        compiler_params=_params(("arbitrary", "arbitrary", "arbitrary"),
                                5 * _nbytes((seq_len, Dv), BF16), 4 * _nbytes((tile, Dv), BF16),
                                2 * _nbytes((Dv, tile), F32), 4 * _nbytes((tile, tile), F32),
                                4 * _nbytes((tile, tile), BF16)),
        name="diff_attention",
    )(slopes, lam_vecs, a, a, a, gain.reshape(1, W))


def _merge_kernel(am_ref, aa_ref, wm_ref, wa_ref, gm_ref, ga_ref, o_ref):
    for cols in _col_blocks(o_ref.shape[1]):
        ym = jnp.dot(am_ref[...], wm_ref[:, cols], preferred_element_type=F32)
        ya = jnp.dot(aa_ref[...], wa_ref[:, cols], preferred_element_type=F32)
        o_ref[:, cols] = (gm_ref[:, cols].astype(F32) * ym + ga_ref[:, cols].astype(F32) * ya).astype(o_ref.dtype)


def _branch_merge(hm, ha, wm, wa, g, tm=1024, tn=1024):
    T, K = hm.shape
    N = wm.shape[1]
    nj = N // tn
    return pl.pallas_call(
        _merge_kernel,
        grid=(T // tm, nj),
        in_specs=[pl.BlockSpec((tm, K), lambda i, j: (i, 0)),
                  pl.BlockSpec((tm, K), lambda i, j: (i, 0)),
                  pl.BlockSpec((K, tn), lambda i, j: (0, j)),
                  pl.BlockSpec((K, tn), lambda i, j: (0, j)),
                  pl.BlockSpec((tm, tn), lambda i, j: (i, j)),
                  pl.BlockSpec((tm, tn), lambda i, j: (i, nj + j))],
        out_specs=pl.BlockSpec((tm, tn), lambda i, j: (i, j)),
        out_shape=jax.ShapeDtypeStruct((T, N), BF16),
        compiler_params=_params(("parallel", "parallel"),
                                4 * _nbytes((tm, K), BF16), 4 * _nbytes((K, tn), BF16),
                                6 * _nbytes((tm, tn), BF16), 2 * _nbytes((tm, tn), F32)),
        name="branch_merge",
    )(hm, ha, wm, wa, g, g)


def _outproj_kernel(mix_ref, w_ref, x_ref, gpost_ref, gpre_ref, x1_ref, h2_ref, *, row_split):
    for rows in _row_blocks(x1_ref.shape[0], row_split):
        y = jnp.dot(mix_ref[rows, :], w_ref[...], preferred_element_type=F32)
        x1 = x_ref[rows, :] + y * _rms_scale(y) * gpost_ref[...]
        x1_ref[rows, :] = x1
        h2_ref[rows, :] = (x1 * _rms_scale(x1) * gpre_ref[...]).astype(h2_ref.dtype)


def _outproj(mix, w, x, g_post, g_pre, tm=512, row_split=4):
    T, K = mix.shape
    D = w.shape[1]
    row = lambda i: (i, 0)
    fixed = lambda i: (0, 0)
    return pl.pallas_call(
        functools.partial(_outproj_kernel, row_split=row_split),
        grid=(T // tm,),
        in_specs=[pl.BlockSpec((tm, K), row),
                  pl.BlockSpec((K, D), fixed),
                  pl.BlockSpec((tm, D), row),
                  pl.BlockSpec((1, D), fixed),
                  pl.BlockSpec((1, D), fixed)],
        out_specs=[pl.BlockSpec((tm, D), row), pl.BlockSpec((tm, D), row)],
        out_shape=[jax.ShapeDtypeStruct((T, D), F32), jax.ShapeDtypeStruct((T, D), BF16)],
        compiler_params=_params(("parallel",),
                                2 * _nbytes((tm, K), BF16), 2 * _nbytes((K, D), BF16),
                                4 * _nbytes((tm, D), F32), 2 * _nbytes((tm, D), BF16),
                                2 * _nbytes((tm, D), F32)),
        name="outproj_norm",
    )(mix, w, x, g_post.reshape(1, D), g_pre.reshape(1, D))


def _gelu_tanh(x):
    return 0.5 * x * (1.0 + jnp.tanh(math.sqrt(2.0 / math.pi) * (x + 0.044715 * (x * x * x))))


def _ffn_up_kernel(h_ref, wg_ref, wv_ref, cwg_ref, cwv_ref, cbg_ref, cbv_ref, o_ref,
                   carry_ref, buf_ref, *, tiles_per_seq, taps, row_split):
    i = pl.program_id(0)
    j = pl.program_id(1)
    first = (i % tiles_per_seq) == 0
    _load_conv_halo(buf_ref.at[0], carry_ref.at[j, 0], first)
    _load_conv_halo(buf_ref.at[1], carry_ref.at[j, 1], first)
    for rows in _row_blocks(o_ref.shape[0], row_split):
        for cols in _col_blocks(o_ref.shape[1]):
            h = h_ref[rows, :]
            gate = _causal_conv_rows(jnp.dot(h, wg_ref[:, cols], preferred_element_type=F32), rows, cols,
                                     buf_ref.at[0], carry_ref.at[j, 0], cwg_ref, cbg_ref, taps)
            val = _causal_conv_rows(jnp.dot(h, wv_ref[:, cols], preferred_element_type=F32), rows, cols,
                                    buf_ref.at[1], carry_ref.at[j, 1], cwv_ref, cbv_ref, taps)
            o_ref[rows, cols] = (_gelu_tanh(gate) * val).astype(o_ref.dtype)


def _ffn_up(h2, w_up, cw, cb, seq_len, tm=1024, tf=512, row_split=1):
    T, K = h2.shape
    F = w_up.shape[1] // 2
    taps = cw.shape[0]
    nj = F // tf
    gate_col = lambda i, j: (0, j)
    val_col = lambda i, j: (0, nj + j)
    cb2 = cb.reshape(1, 2 * F)
    return pl.pallas_call(
        functools.partial(_ffn_up_kernel, tiles_per_seq=seq_len // tm, taps=taps, row_split=row_split),
        grid=(T // tm, nj),
        in_specs=[pl.BlockSpec((tm, K), lambda i, j: (i, 0)),
                  pl.BlockSpec((K, tf), gate_col),
                  pl.BlockSpec((K, tf), val_col),
                  pl.BlockSpec((taps, tf), gate_col),
                  pl.BlockSpec((taps, tf), val_col),
                  pl.BlockSpec((1, tf), gate_col),
                  pl.BlockSpec((1, tf), val_col)],
        out_specs=pl.BlockSpec((tm, tf), lambda i, j: (i, j)),
        out_shape=jax.ShapeDtypeStruct((T, F), BF16),
        scratch_shapes=[pltpu.VMEM((nj, 2, V7X_SUBLANES, tf), F32),
                        pltpu.VMEM((2, tm + V7X_SUBLANES, tf), F32)],
        compiler_params=_params(("arbitrary", "arbitrary"),
                                2 * _nbytes((tm, K), BF16), 4 * _nbytes((K, tf), BF16),
                                2 * _nbytes((tm, tf), BF16), 4 * _nbytes((tm, tf), F32)),
        name="ffn_up_conv_gelu",
    )(h2, w_up, w_up, cw, cw, cb2, cb2)


def _ffn_down_kernel(a_ref, w_ref, x1_ref, g_ref, o_ref, acc_ref):
    k = pl.program_id(1)

    @pl.when(k == 0)
    def _():
        acc_ref[...] = jnp.zeros_like(acc_ref)

    for cols in _col_blocks(acc_ref.shape[1]):
        acc_ref[:, cols] += jnp.dot(a_ref[...], w_ref[:, cols], preferred_element_type=F32)

    @pl.when(k == pl.num_programs(1) - 1)
    def _():
        y = acc_ref[...]
        o_ref[...] = x1_ref[...] + y * _rms_scale(y) * g_ref[...]


def _ffn_down(act, w, x1, g, tm=512, tk=2816):
    T, F = act.shape
    assert F % tk == 0 and tk % V7X_MXU_WIDTH == 0
    D = w.shape[1]
    return pl.pallas_call(
        _ffn_down_kernel,
        grid=(T // tm, F // tk),
        in_specs=[pl.BlockSpec((tm, tk), lambda i, k: (i, k)),
                  pl.BlockSpec((tk, D), lambda i, k: (k, 0)),
                  pl.BlockSpec((tm, D), lambda i, k: (i, 0)),
                  pl.BlockSpec((1, D), lambda i, k: (0, 0))],
        out_specs=pl.BlockSpec((tm, D), lambda i, k: (i, 0)),
        out_shape=jax.ShapeDtypeStruct((T, D), F32),
        scratch_shapes=[pltpu.VMEM((tm, D), F32)],
        compiler_params=_params(("parallel", "arbitrary"),
                                2 * _nbytes((tm, tk), BF16), 2 * _nbytes((tk, D), BF16),
                                5 * _nbytes((tm, D), F32)),
        name="ffn_down_norm",
    )(act, w, x1, g.reshape(1, D))


def _layer(x, l, g_pre_mix, w_in, b_gates, m_conv_w, m_conv_b, m_head_norm,
           lambda_q1, lambda_k1, lambda_q2, lambda_k2, a_head_norm,
           w_branch_m, w_branch_a, w_out, g_post_mix, g_pre_ffn,
           w_up, ffn_conv_w, ffn_conv_b, w_down, g_post_ffn, batch, seq_len):
    D = x.shape[1]
    mw = M_HEADS * M_HEAD_DIM
    aw = A_HEADS * A_V_DIM
    n_gate = 2 * M_HEADS
    pre = 4 * mw
    post0 = pre + n_gate
    lam_init = 0.8 - 0.6 * math.exp(-0.3 * l)

    wt = w_in.T
    wt_pre = _cast_rows(wt, 0, 0, pre)
    wt_post = _cast_rows(wt, pre, n_gate, 3 * aw + 2 * D)
    wt_gate = jnp.pad(wt[pre:post0], ((0, V7X_LANES - n_gate), (0, 0))).astype(BF16)
    b_gate = jnp.pad(b_gates.astype(F32), (0, V7X_LANES - n_gate))
    a_scale = jnp.concatenate([jnp.full((aw,), A_QK_DIM ** -0.5 * LOG2E, F32), jnp.ones((2 * aw,), F32)])
    lam_vecs = jnp.stack([lambda_q1, lambda_k1, lambda_q2, lambda_k2]).astype(F32)
    zeros = lambda n: jnp.zeros((n,), F32)
    ones = lambda n: jnp.ones((n,), F32)

    h = _rmsnorm(x, g_pre_mix)
    qk = _proj_conv_silu(h, wt_pre, 2 * mw, m_conv_w, m_conv_b, seq_len)
    v = _proj_act(h, wt_pre, 2 * mw, mw, ones(mw), zeros(mw), "identity", BF16)
    og = _proj_act(h, wt_pre, 3 * mw, mw, ones(mw), zeros(mw), "sigmoid", BF16)
    gates = _proj_act(h, wt_gate, 0, V7X_LANES, ones(V7X_LANES), b_gate, "identity", F32)
    a = _proj_act(h, wt_post, 0, 3 * aw, a_scale, zeros(3 * aw), "identity", BF16)
    g = _proj_act(h, wt_post, 3 * aw, 2 * D, ones(2 * D), zeros(2 * D), "sigmoid", BF16)

    hm = _mlstm(qk, v, og, gates, m_head_norm, batch, seq_len)
    ha = _diff_attention(a, lam_vecs, a_head_norm, lam_init, batch, seq_len)

    mix = _branch_merge(hm, ha, w_branch_m.astype(BF16), w_branch_a.astype(BF16), g)
    x1, h2 = _outproj(mix, w_out.astype(BF16), x, g_post_mix, g_pre_ffn)
    act = _ffn_up(h2, w_up.astype(BF16), ffn_conv_w, ffn_conv_b, seq_len)
    return _ffn_down(act, w_down.astype(BF16), x1, g_post_ffn)


def kernel(x, g_pre_mix, w_in, b_gates, m_conv_w, m_conv_b, m_head_norm, lambda_q1, lambda_k1, lambda_q2, lambda_k2, a_head_norm, w_branch_m, w_branch_a, w_out, g_post_mix, g_pre_ffn, w_up, ffn_conv_w, ffn_conv_b, w_down, g_post_ffn):
    B, S, D = x.shape
    layers = (g_pre_mix, w_in, b_gates, m_conv_w, m_conv_b, m_head_norm, lambda_q1, lambda_k1,
              lambda_q2, lambda_k2, a_head_norm, w_branch_m, w_branch_a, w_out, g_post_mix,
              g_pre_ffn, w_up, ffn_conv_w, ffn_conv_b, w_down, g_post_ffn)
    y = x.reshape(B * S, D)
    for l in range(w_in.shape[0]):
        y = _layer(y, l, *[p[l] for p in layers], batch=B, seq_len=S)
    return y.reshape(B, S, D)
```

```python
import functools
import math

import numpy as np
import jax
import jax.numpy as jnp
from jax import lax
from jax.experimental import pallas as pl
from jax.experimental.pallas import tpu as pltpu

F32 = jnp.float32
BF16 = jnp.bfloat16

EPS = 1e-6
CHUNK = 64
M_HEADS = 8
M_HEAD_DIM = 256
A_HEADS = 8
A_QK_DIM = 128
A_V_DIM = 256
ALIBI_SLOPES = 2.0 ** (-8.0 * np.arange(1, A_HEADS + 1) / A_HEADS)

V7X_LANES = 128
V7X_SUBLANES = 8
V7X_MXU_WIDTH = 256
V7X_NUM_MXU = 2
V7X_SCOPED_VMEM_BYTES = 60000 * 1024
COMPILER_TEMP_BYTES = 16 * 1024 * 1024

MLSTM_CHUNK = V7X_MXU_WIDTH
ATTN_TILE = 512
ATTN_VT_PAD = 16
UNDERFLOW_EXP2 = 136.0
ATTN_BOUND_MARGIN = 1.0
MASK_VALUE = -1e30
LOG2E = math.log2(math.e)


def _vmem_limit(*buffer_bytes):
    return int(min(sum(buffer_bytes) + COMPILER_TEMP_BYTES, V7X_SCOPED_VMEM_BYTES))


def _nbytes(shape, dtype):
    return int(np.prod(shape)) * jnp.dtype(dtype).itemsize


def _params(semantics, *buffer_bytes):
    return pltpu.CompilerParams(dimension_semantics=semantics,
                                vmem_limit_bytes=_vmem_limit(*buffer_bytes))


def _rms_scale(y):
    return lax.rsqrt(jnp.mean(y * y, axis=-1, keepdims=True) + EPS)


def _sigmoid(x):
    return 0.5 * jnp.tanh(0.5 * x) + 0.5


def _col_blocks(width):
    sub = V7X_NUM_MXU * V7X_MXU_WIDTH
    if width % sub:
        return [slice(0, width)]
    return [slice(c, c + sub) for c in range(0, width, sub)]


def _rmsnorm_kernel(x_ref, g_ref, o_ref):
    x = x_ref[...]
    o_ref[...] = (x * _rms_scale(x) * g_ref[...]).astype(o_ref.dtype)


def _rmsnorm(x, g, tm=512):
    T, D = x.shape
    return pl.pallas_call(
        _rmsnorm_kernel,
        grid=(T // tm,),
        in_specs=[pl.BlockSpec((tm, D), lambda i: (i, 0)),
                  pl.BlockSpec((1, D), lambda i: (0, 0))],
        out_specs=pl.BlockSpec((tm, D), lambda i: (i, 0)),
        out_shape=jax.ShapeDtypeStruct((T, D), BF16),
        compiler_params=_params(("parallel",), 2 * _nbytes((tm, D), F32), 2 * _nbytes((tm, D), BF16)),
        name="rmsnorm",
    )(x, g.reshape(1, D))


def _cast_rows_kernel(main_ref, next_ref, o_ref, *, lead):
    if lead:
        rows = jnp.concatenate([main_ref[lead:, :], next_ref[...]], axis=0)
    else:
        rows = main_ref[...]
    o_ref[...] = rows.astype(o_ref.dtype)


def _cast_rows(wt, row0, lead, n_rows, tr=512):
    K = wt.shape[1]
    nxt = max(lead, V7X_SUBLANES)
    assert row0 % tr == 0 and n_rows % tr == 0 and lead % V7X_SUBLANES == 0 and tr % nxt == 0
    i0 = row0 // tr
    return pl.pallas_call(
        functools.partial(_cast_rows_kernel, lead=lead),
        grid=(n_rows // tr,),
        in_specs=[pl.BlockSpec((tr, K), lambda i: (i0 + i, 0)),
                  pl.BlockSpec((nxt, K), lambda i: ((i0 + i + 1) * (tr // nxt), 0))],
        out_specs=pl.BlockSpec((tr, K), lambda i: (i, 0)),
        out_shape=jax.ShapeDtypeStruct((n_rows, K), BF16),
        compiler_params=_params(("parallel",), 2 * _nbytes((tr, K), F32), 2 * _nbytes((tr, K), BF16)),
        name="cast_rows",
    )(wt, wt)


def _load_conv_halo(buf_ref, carry_ref, first_tile):
    halo = V7X_SUBLANES

    @pl.when(first_tile)
    def _():
        buf_ref[0:halo, :] = jnp.zeros((halo, buf_ref.shape[1]), F32)

    @pl.when(jnp.logical_not(first_tile))
    def _():
        buf_ref[0:halo, :] = carry_ref[...]


def _causal_conv_rows(acc, rows, cols, buf_ref, carry_ref, cw_ref, cb_ref, taps):
    halo = V7X_SUBLANES
    r0, r1 = rows.start, rows.stop
    buf_ref[halo + r0:halo + r1, cols] = acc
    if r1 == buf_ref.shape[0] - halo:
        carry_ref[:, cols] = acc[r1 - r0 - halo:, :]
    y = cb_ref[:, cols] + cw_ref[taps - 1:taps, cols] * acc
    window = buf_ref[r0:halo + r1, cols]
    for k in range(taps - 1):
        back = taps - 1 - k
        y = y + cw_ref[k:k + 1, cols] * pltpu.roll(window, back, 0)[halo:]
    return y


def _row_blocks(tm, split):
    step = tm // split
    return [slice(r, r + step) for r in range(0, tm, step)]


def _dot_nt(x, wt):
    return lax.dot_general(x, wt, (((1,), (1,)), ((), ())), preferred_element_type=F32)


def _proj_conv_silu_kernel(h_ref, wt_ref, cw_ref, cb_ref, o_ref, carry_ref, buf_ref, *,
                           tiles_per_seq, taps, row_split):
    i = pl.program_id(0)
    j = pl.program_id(1)
    _load_conv_halo(buf_ref, carry_ref.at[j], (i % tiles_per_seq) == 0)
    for rows in _row_blocks(o_ref.shape[0], row_split):
        for cols in _col_blocks(o_ref.shape[1]):
            acc = _dot_nt(h_ref[rows, :], wt_ref[cols, :])
            y = _causal_conv_rows(acc, rows, cols, buf_ref, carry_ref.at[j], cw_ref, cb_ref, taps)
            o_ref[rows, cols] = (y * _sigmoid(y)).astype(o_ref.dtype)


def _proj_conv_silu(h, wt, n_cols, cw, cb, seq_len, tm=1024, tn=1024, row_split=1):
    T, K = h.shape
    taps = cw.shape[0]
    nj = n_cols // tn
    kern = functools.partial(_proj_conv_silu_kernel, tiles_per_seq=seq_len // tm, taps=taps,
                             row_split=row_split)
    return pl.pallas_call(
        kern,
        grid=(T // tm, nj),
        in_specs=[pl.BlockSpec((tm, K), lambda i, j: (i, 0)),
                  pl.BlockSpec((tn, K), lambda i, j: (j, 0)),
                  pl.BlockSpec((taps, tn), lambda i, j: (0, j)),
                  pl.BlockSpec((1, tn), lambda i, j: (0, j))],
        out_specs=pl.BlockSpec((tm, tn), lambda i, j: (i, j)),
        out_shape=jax.ShapeDtypeStruct((T, n_cols), BF16),
        scratch_shapes=[pltpu.VMEM((nj, V7X_SUBLANES, tn), F32),
                        pltpu.VMEM((tm + V7X_SUBLANES, tn), F32)],
        compiler_params=_params(("arbitrary", "arbitrary"),
                                2 * _nbytes((tm, K), BF16), 2 * _nbytes((K, tn), BF16),
                                2 * _nbytes((tm, tn), BF16), 2 * _nbytes((tm, tn), F32)),
        name="proj_conv_silu",
    )(h, wt, cw, cb.reshape(1, n_cols))


def _proj_act_kernel(h_ref, wt_ref, sc_ref, b_ref, o_ref, *, act, affine):
    for cols in _col_blocks(o_ref.shape[1]):
        acc = _dot_nt(h_ref[...], wt_ref[cols, :])
        if affine:
            acc = acc * sc_ref[:, cols] + b_ref[:, cols]
        if act == "sigmoid":
            acc = _sigmoid(acc)
        o_ref[:, cols] = acc.astype(o_ref.dtype)


def _proj_act(h, wt, row0, n_cols, scale, bias, act, out_dtype, tm=1024, tn=1024):
    T, K = h.shape
    tn = min(tn, n_cols)
    j0 = row0 // tn
    affine = scale is not None
    if not affine:
        scale, bias = jnp.ones((n_cols,), F32), jnp.zeros((n_cols,), F32)
    return pl.pallas_call(
        functools.partial(_proj_act_kernel, act=act, affine=affine),
        grid=(T // tm, n_cols // tn),
        in_specs=[pl.BlockSpec((tm, K), lambda i, j: (i, 0)),
                  pl.BlockSpec((tn, K), lambda i, j: (j0 + j, 0)),
                  pl.BlockSpec((1, tn), lambda i, j: (0, j)),
                  pl.BlockSpec((1, tn), lambda i, j: (0, j))],
        out_specs=pl.BlockSpec((tm, tn), lambda i, j: (i, j)),
        out_shape=jax.ShapeDtypeStruct((T, n_cols), out_dtype),
        compiler_params=_params(("parallel", "parallel"),
                                2 * _nbytes((tm, K), BF16), 2 * _nbytes((K, tn), BF16),
                                2 * _nbytes((tm, tn), out_dtype), _nbytes((tm, tn), F32)),
        name="proj_" + act,
    )(h, wt, scale.reshape(1, n_cols), bias.reshape(1, n_cols))


def _mlstm_kernel(q_ref, k_ref, v_ref, og_ref, g_ref, gain_ref, o_ref, c_ref, n_ref, m_ref, *,
                  heads, head_dim):
    L = q_ref.shape[0]
    H, Dh = heads, head_dim
    q_scale = Dh ** -0.5

    @pl.when(pl.program_id(1) == 0)
    def _():
        c_ref[...] = jnp.zeros_like(c_ref)
        n_ref[...] = jnp.zeros_like(n_ref)
        m_ref[...] = jnp.zeros_like(m_ref)

    gates = g_ref[...]
    logf = jnp.minimum(gates, 0.0) - jnp.log1p(jnp.exp(-jnp.abs(gates)))
    srow = lax.broadcasted_iota(jnp.int32, (L, L), 0)
    tcol = lax.broadcasted_iota(jnp.int32, (L, L), 1)
    causal = srow <= tcol
    cum = jnp.dot((tcol <= srow).astype(F32), logf, precision=lax.Precision.HIGHEST,
                  preferred_element_type=F32)
    u_all = gates - pltpu.roll(cum, V7X_LANES - H, 1)
    u_all_t = u_all.T
    cum_t = cum.T

    for h in range(H):
        hs = slice(h * Dh, (h + 1) * Dh)
        u_row = u_all_t[h:h + 1, :]
        b_row = cum_t[H + h:H + h + 1, :]
        u_keys = jnp.broadcast_to(u_all[:, h:h + 1], (L, L))
        m_prev = m_ref[h, 0:1, 0:1]
        cu = jnp.max(jnp.where(causal, u_keys, -jnp.inf), axis=0, keepdims=True)
        mm = jnp.maximum(m_prev, cu)
        w_intra = jnp.where(causal, jnp.exp(u_keys - mm), 0.0) * q_scale
        w_inter = jnp.exp(m_prev - mm) * q_scale

        qh = q_ref[:, hs]
        kh = k_ref[:, hs]
        vt = v_ref[:, hs].T
        sqk = _dot_nt(kh, qh) * w_intra
        num = (jnp.dot(vt, sqk.astype(BF16), preferred_element_type=F32)
               + w_inter * _dot_nt(c_ref[h].astype(BF16), qh))
        qn = _dot_nt(n_ref[h].astype(BF16), qh)[0:1, :]
        den = jnp.sum(sqk, axis=0, keepdims=True) + w_inter * qn
        hh = num / jnp.maximum(jnp.abs(den), jnp.exp(-(b_row + mm)))
        scale = lax.rsqrt(jnp.mean(hh * hh, axis=0, keepdims=True) + EPS)
        y = (hh * scale).T * gain_ref[:, hs]
        o_ref[:, hs] = (y * og_ref[:, hs].astype(F32)).astype(o_ref.dtype)

        mm_last = mm[:, L - 1:L]
        ws = jnp.exp(u_row - mm_last)
        decay = jnp.exp(m_prev - mm_last)
        c_ref[h] = decay * c_ref[h] + jnp.dot((vt.astype(F32) * ws).astype(BF16), kh,
                                              preferred_element_type=F32)
        ws_rows = jnp.broadcast_to(ws, (V7X_SUBLANES, L)).astype(BF16)
        n_ref[h] = decay * n_ref[h] + jnp.dot(ws_rows, kh, preferred_element_type=F32)
        m_ref[h] = jnp.broadcast_to(b_row[:, L - 1:L] + mm_last, m_ref.shape[1:])


def _mlstm(qk, v, og, gates, gain, batch, seq_len):
    T = qk.shape[0]
    H, Dh, L = M_HEADS, M_HEAD_DIM, MLSTM_CHUNK
    W = H * Dh
    nt = seq_len // L
    row_blk = lambda b, t: (b * nt + t, 0)
    return pl.pallas_call(
        functools.partial(_mlstm_kernel, heads=H, head_dim=Dh),
        grid=(batch, nt),
        in_specs=[pl.BlockSpec((L, W), row_blk),
                  pl.BlockSpec((L, W), lambda b, t: (b * nt + t, 1)),
                  pl.BlockSpec((L, W), row_blk),
                  pl.BlockSpec((L, W), row_blk),
                  pl.BlockSpec((L, V7X_LANES), row_blk),
                  pl.BlockSpec((1, W), lambda b, t: (0, 0))],
        out_specs=pl.BlockSpec((L, W), row_blk),
        out_shape=jax.ShapeDtypeStruct((T, W), BF16),
        scratch_shapes=[pltpu.VMEM((H, Dh, Dh), F32),
                        pltpu.VMEM((H, V7X_SUBLANES, Dh), F32),
                        pltpu.VMEM((H, V7X_SUBLANES, V7X_LANES), F32)],
        compiler_params=_params(("arbitrary", "arbitrary"),
                                10 * _nbytes((L, W), BF16), _nbytes((H, Dh, Dh), F32)),
        name="mlstm",
    )(qk, qk, v, og, gates, gain.reshape(1, W))


def _attn_kernel(slopes_ref, lam_ref, q_ref, k_ref, v_ref, gain_ref, o_ref,
                 vt_ref, acc_ref, m_ref, s_ref, p_ref, al_ref, kmax_ref, *, tile, lam_init):
    h = pl.program_id(1)
    qi = pl.program_id(2)
    dk = A_QK_DIM
    dv = A_V_DIM
    slope = slopes_ref[h] * LOG2E
    seq = k_ref.shape[0]

    def max_row_norm_sq(x):
        sq = x.astype(F32)
        sq = sq * sq
        return jnp.maximum(jnp.max(jnp.sum(sq[:, 0:dk], axis=1, keepdims=True)),
                           jnp.max(jnp.sum(sq[:, dk:2 * dk], axis=1, keepdims=True)))

    @pl.when(qi == 0)
    def _():
        kmax = jnp.float32(0.0)
        for c in range(seq // tile):
            rows = slice(c * tile, (c + 1) * tile)
            vt_ref[0:dv, rows] = v_ref[rows, :].T
            kmax = jnp.maximum(kmax, max_row_norm_sq(k_ref[rows, :]))
        extra = lax.broadcasted_iota(jnp.int32, (ATTN_VT_PAD, seq), 0)
        vt_ref[dv:dv + ATTN_VT_PAD, :] = jnp.where(extra == 0, 1.0, 0.0).astype(BF16)
        kmax_ref[0] = kmax

    m_ref[...] = jnp.full(m_ref.shape, MASK_VALUE, F32)
    acc_ref[...] = jnp.zeros_like(acc_ref)

    krow = lax.broadcasted_iota(jnp.int32, (tile, tile), 0)
    qcol = lax.broadcasted_iota(jnp.int32, (tile, tile), 1)
    rel = (qcol - krow).astype(F32)
    bias_full = -slope * rel

    def scores(j, slot):
        kt = k_ref[pl.ds(pl.multiple_of(j * tile, tile), tile), :]
        for c in range(2):
            s_ref[slot, c] = lax.dot_general(kt[:, c * dk:(c + 1) * dk], q_ref[:, c * dk:(c + 1) * dk],
                                             (((1,), (1,)), ((), ())), preferred_element_type=F32)

    def softmax_step(slot, bias, shift):
        for c in range(2):
            m_prev = m_ref[c]
            m_new = jnp.maximum(m_prev, jnp.max(s_ref[slot, c] + bias, axis=0, keepdims=True) + shift)
            m_ref[c] = m_new
            p_ref[slot, c] = jnp.exp2((bias - (m_new - shift)) + s_ref[slot, c]).astype(BF16)
            al_ref[slot, c] = jnp.exp2(m_prev - m_new)

    def accumulate(j, slot):
        vt = vt_ref[:, pl.ds(pl.multiple_of(j * tile, tile), tile)]
        for c in range(2):
            acc_ref[c] = al_ref[slot, c] * acc_ref[c] + jnp.dot(vt, p_ref[slot, c], preferred_element_type=F32)

    def tile_of(t):
        return jnp.clip(qi - 1 - t, 0, qi)

    def shift_of(t, n_valid):
        return jnp.where(t < n_valid, -slope * jnp.asarray((t + 1) * tile).astype(F32), MASK_VALUE)

    scores(qi, 0)
    scores(tile_of(0), 1)
    allowed = (krow // CHUNK) <= (qcol // CHUNK)
    softmax_step(0, jnp.where(allowed, -slope * jnp.abs(rel), MASK_VALUE), 0.0)
    softmax_step(1, bias_full, shift_of(0, qi))

    qk_bound = jnp.sqrt(max_row_norm_sq(q_ref[...]) * kmax_ref[0]) + ATTN_BOUND_MARGIN
    reach = (qk_bound + UNDERFLOW_EXP2 - jnp.min(m_ref[...])) / (slope * tile)
    n_keep = jnp.minimum(qi, jnp.clip(reach, 0.0, 1e6).astype(jnp.int32) + 1)
    pairs = lax.div(jnp.maximum(n_keep - 1, 0) + 1, 2)

    @pl.when(pairs > 0)
    def _():
        scores(tile_of(1), 0)
        scores(tile_of(2), 1)

    def stage(a, lookahead):
        accumulate(tile_of(a - 2), 0)
        accumulate(tile_of(a - 1), 1)
        softmax_step(0, bias_full, shift_of(a, n_keep))
        if lookahead:
            scores(tile_of(a + 2), 0)
        softmax_step(1, bias_full, shift_of(a + 1, n_keep))
        if lookahead:
            scores(tile_of(a + 3), 1)

    def pair(g, carry):
        stage(1 + 2 * g, True)
        return carry

    lax.fori_loop(0, pairs - 1, pair, 0)

    @pl.when(pairs > 0)
    def _():
        stage(2 * pairs - 1, False)

    last = 2 * pairs + 1
    accumulate(tile_of(last - 2), 0)
    accumulate(tile_of(last - 1), 1)

    lam_vec = lam_ref[...]
    lam = (jnp.exp(jnp.sum(lam_vec[0:1] * lam_vec[1:2], axis=1, keepdims=True))
           - jnp.exp(jnp.sum(lam_vec[2:3] * lam_vec[3:4], axis=1, keepdims=True)) + lam_init)
    o = (acc_ref[0, 0:dv] / acc_ref[0, dv:dv + 1]
         - lam * (acc_ref[1, 0:dv] / acc_ref[1, dv:dv + 1]))
    scale = lax.rsqrt(jnp.mean(o * o, axis=0, keepdims=True) + EPS) * (1.0 - lam_init)
    o_ref[...] = ((o * scale).T * gain_ref[...]).astype(o_ref.dtype)


def _diff_attention(a, lam_vecs, gain, lam_init, batch, seq_len):
    T = a.shape[0]
    H, Dv = A_HEADS, A_V_DIM
    W = H * Dv
    tile = ATTN_TILE
    assert tile % CHUNK == 0 and seq_len % tile == 0
    nq = seq_len // tile
    slopes = jnp.asarray(ALIBI_SLOPES, F32)
    return pl.pallas_call(
        functools.partial(_attn_kernel, tile=tile, lam_init=lam_init),
        grid=(batch, H, nq),
        in_specs=[pl.BlockSpec(memory_space=pltpu.SMEM),
                  pl.BlockSpec((4, A_QK_DIM), lambda b, h, i: (0, 0)),
                  pl.BlockSpec((tile, Dv), lambda b, h, i: (b * nq + i, h)),
                  pl.BlockSpec((seq_len, Dv), lambda b, h, i: (b, H + h)),
                  pl.BlockSpec((seq_len, Dv), lambda b, h, i: (b, 2 * H + h)),
                  pl.BlockSpec((1, Dv), lambda b, h, i: (0, h))],
        out_specs=pl.BlockSpec((tile, Dv), lambda b, h, i: (b * nq + i, h)),
        out_shape=jax.ShapeDtypeStruct((T, W), BF16),
        scratch_shapes=[pltpu.VMEM((Dv + ATTN_VT_PAD, seq_len), BF16),
                        pltpu.VMEM((2, Dv + ATTN_VT_PAD, tile), F32),
                        pltpu.VMEM((2, 1, tile), F32),
                        pltpu.VMEM((2, 2, tile, tile), F32),
                        pltpu.VMEM((2, 2, tile, tile), BF16),
                        pltpu.VMEM((2, 2, 1, tile), F32),
                        pltpu.SMEM((1,), F32)],
        compiler_params=_params(("arbitrary", "arbitrary", "arbitrary"),
                                5 * _nbytes((seq_len, Dv), BF16), 4 * _nbytes((tile, Dv), BF16),
                                2 * _nbytes((Dv, tile), F32), 4 * _nbytes((tile, tile), F32),
                                4 * _nbytes((tile, tile), BF16)),
        name="diff_attention",
    )(slopes, lam_vecs, a, a, a, gain.reshape(1, W))


def _merge_kernel(am_ref, aa_ref, wm_ref, wa_ref, gm_ref, ga_ref, o_ref):
    for cols in _col_blocks(o_ref.shape[1]):
        ym = jnp.dot(am_ref[...], wm_ref[:, cols], preferred_element_type=F32)
        ya = jnp.dot(aa_ref[...], wa_ref[:, cols], preferred_element_type=F32)
        o_ref[:, cols] = (gm_ref[:, cols].astype(F32) * ym + ga_ref[:, cols].astype(F32) * ya).astype(o_ref.dtype)


def _branch_merge(hm, ha, wm, wa, g, tm=1024, tn=1024):
    T, K = hm.shape
    N = wm.shape[1]
    nj = N // tn
    return pl.pallas_call(
        _merge_kernel,
        grid=(T // tm, nj),
        in_specs=[pl.BlockSpec((tm, K), lambda i, j: (i, 0)),
                  pl.BlockSpec((tm, K), lambda i, j: (i, 0)),
                  pl.BlockSpec((K, tn), lambda i, j: (0, j)),
                  pl.BlockSpec((K, tn), lambda i, j: (0, j)),
                  pl.BlockSpec((tm, tn), lambda i, j: (i, j)),
                  pl.BlockSpec((tm, tn), lambda i, j: (i, nj + j))],
        out_specs=pl.BlockSpec((tm, tn), lambda i, j: (i, j)),
        out_shape=jax.ShapeDtypeStruct((T, N), BF16),
        compiler_params=_params(("parallel", "parallel"),
                                4 * _nbytes((tm, K), BF16), 4 * _nbytes((K, tn), BF16),
                                6 * _nbytes((tm, tn), BF16), 2 * _nbytes((tm, tn), F32)),
        name="branch_merge",
    )(hm, ha, wm, wa, g, g)


def _outproj_kernel(mix_ref, w_ref, x_ref, gpost_ref, gpre_ref, x1_ref, h2_ref, *, row_split):
    for rows in _row_blocks(x1_ref.shape[0], row_split):
        y = jnp.dot(mix_ref[rows, :], w_ref[...], preferred_element_type=F32)
        x1 = x_ref[rows, :] + y * _rms_scale(y) * gpost_ref[...]
        x1_ref[rows, :] = x1
        h2_ref[rows, :] = (x1 * _rms_scale(x1) * gpre_ref[...]).astype(h2_ref.dtype)


def _outproj(mix, w, x, g_post, g_pre, tm=512, row_split=4):
    T, K = mix.shape
    D = w.shape[1]
    row = lambda i: (i, 0)
    fixed = lambda i: (0, 0)
    return pl.pallas_call(
        functools.partial(_outproj_kernel, row_split=row_split),
        grid=(T // tm,),
        in_specs=[pl.BlockSpec((tm, K), row),
                  pl.BlockSpec((K, D), fixed),
                  pl.BlockSpec((tm, D), row),
                  pl.BlockSpec((1, D), fixed),
                  pl.BlockSpec((1, D), fixed)],
        out_specs=[pl.BlockSpec((tm, D), row), pl.BlockSpec((tm, D), row)],
        out_shape=[jax.ShapeDtypeStruct((T, D), F32), jax.ShapeDtypeStruct((T, D), BF16)],
        compiler_params=_params(("parallel",),
                                2 * _nbytes((tm, K), BF16), 2 * _nbytes((K, D), BF16),
                                4 * _nbytes((tm, D), F32), 2 * _nbytes((tm, D), BF16),
                                2 * _nbytes((tm, D), F32)),
        name="outproj_norm",
    )(mix, w, x, g_post.reshape(1, D), g_pre.reshape(1, D))


def _gelu_tanh(x):
    return 0.5 * x * (1.0 + jnp.tanh(math.sqrt(2.0 / math.pi) * (x + 0.044715 * (x * x * x))))


def _ffn_up_kernel(h_ref, wg_ref, wv_ref, cwg_ref, cwv_ref, cbg_ref, cbv_ref, o_ref,
                   carry_ref, buf_ref, *, tiles_per_seq, taps, row_split):
    i = pl.program_id(0)
    j = pl.program_id(1)
    first = (i % tiles_per_seq) == 0
    _load_conv_halo(buf_ref.at[0], carry_ref.at[j, 0], first)
    _load_conv_halo(buf_ref.at[1], carry_ref.at[j, 1], first)
    for rows in _row_blocks(o_ref.shape[0], row_split):
        for cols in _col_blocks(o_ref.shape[1]):
            h = h_ref[rows, :]
            gate = _causal_conv_rows(jnp.dot(h, wg_ref[:, cols], preferred_element_type=F32), rows, cols,
                                     buf_ref.at[0], carry_ref.at[j, 0], cwg_ref, cbg_ref, taps)
            val = _causal_conv_rows(jnp.dot(h, wv_ref[:, cols], preferred_element_type=F32), rows, cols,
                                    buf_ref.at[1], carry_ref.at[j, 1], cwv_ref, cbv_ref, taps)
            o_ref[rows, cols] = (_gelu_tanh(gate) * val).astype(o_ref.dtype)


def _ffn_up(h2, w_up, cw, cb, seq_len, tm=1024, tf=512, row_split=1):
    T, K = h2.shape
    F = w_up.shape[1] // 2
    taps = cw.shape[0]
    nj = F // tf
    gate_col = lambda i, j: (0, j)
    val_col = lambda i, j: (0, nj + j)
    cb2 = cb.reshape(1, 2 * F)
    return pl.pallas_call(
        functools.partial(_ffn_up_kernel, tiles_per_seq=seq_len // tm, taps=taps, row_split=row_split),
        grid=(T // tm, nj),
        in_specs=[pl.BlockSpec((tm, K), lambda i, j: (i, 0)),
                  pl.BlockSpec((K, tf), gate_col),
                  pl.BlockSpec((K, tf), val_col),
                  pl.BlockSpec((taps, tf), gate_col),
                  pl.BlockSpec((taps, tf), val_col),
                  pl.BlockSpec((1, tf), gate_col),
                  pl.BlockSpec((1, tf), val_col)],
        out_specs=pl.BlockSpec((tm, tf), lambda i, j: (i, j)),
        out_shape=jax.ShapeDtypeStruct((T, F), BF16),
        scratch_shapes=[pltpu.VMEM((nj, 2, V7X_SUBLANES, tf), F32),
                        pltpu.VMEM((2, tm + V7X_SUBLANES, tf), F32)],
        compiler_params=_params(("arbitrary", "arbitrary"),
                                2 * _nbytes((tm, K), BF16), 4 * _nbytes((K, tf), BF16),
                                2 * _nbytes((tm, tf), BF16), 4 * _nbytes((tm, tf), F32)),
        name="ffn_up_conv_gelu",
    )(h2, w_up, w_up, cw, cw, cb2, cb2)


def _ffn_down_kernel(a_ref, w_ref, x1_ref, g_ref, o_ref, acc_ref):
    k = pl.program_id(1)

    @pl.when(k == 0)
    def _():
        acc_ref[...] = jnp.zeros_like(acc_ref)

    for cols in _col_blocks(acc_ref.shape[1]):
        acc_ref[:, cols] += jnp.dot(a_ref[...], w_ref[:, cols], preferred_element_type=F32)

    @pl.when(k == pl.num_programs(1) - 1)
    def _():
        y = acc_ref[...]
        o_ref[...] = x1_ref[...] + y * _rms_scale(y) * g_ref[...]


def _ffn_down(act, w, x1, g, tm=512, tk=2816):
    T, F = act.shape
    assert F % tk == 0 and tk % V7X_MXU_WIDTH == 0
    D = w.shape[1]
    return pl.pallas_call(
        _ffn_down_kernel,
        grid=(T // tm, F // tk),
        in_specs=[pl.BlockSpec((tm, tk), lambda i, k: (i, k)),
                  pl.BlockSpec((tk, D), lambda i, k: (k, 0)),
                  pl.BlockSpec((tm, D), lambda i, k: (i, 0)),
                  pl.BlockSpec((1, D), lambda i, k: (0, 0))],
        out_specs=pl.BlockSpec((tm, D), lambda i, k: (i, 0)),
        out_shape=jax.ShapeDtypeStruct((T, D), F32),
        scratch_shapes=[pltpu.VMEM((tm, D), F32)],
        compiler_params=_params(("parallel", "arbitrary"),
                                2 * _nbytes((tm, tk), BF16), 2 * _nbytes((tk, D), BF16),
                                5 * _nbytes((tm, D), F32)),
        name="ffn_down_norm",
    )(act, w, x1, g.reshape(1, D))


def _layer(x, l, g_pre_mix, w_in, b_gates, m_conv_w, m_conv_b, m_head_norm,
           lambda_q1, lambda_k1, lambda_q2, lambda_k2, a_head_norm,
           w_branch_m, w_branch_a, w_out, g_post_mix, g_pre_ffn,
           w_up, ffn_conv_w, ffn_conv_b, w_down, g_post_ffn, batch, seq_len):
    D = x.shape[1]
    mw = M_HEADS * M_HEAD_DIM
    aw = A_HEADS * A_V_DIM
    n_gate = 2 * M_HEADS
    pre = 4 * mw
    post0 = pre + n_gate
    lam_init = 0.8 - 0.6 * math.exp(-0.3 * l)

    wt = w_in.T
    wt_pre = _cast_rows(wt, 0, 0, pre)
    wt_post = _cast_rows(wt, pre, n_gate, 3 * aw + 2 * D)
    wt_gate = jnp.pad(wt[pre:post0], ((0, V7X_LANES - n_gate), (0, 0))).astype(BF16)
    b_gate = jnp.pad(b_gates.astype(F32), (0, V7X_LANES - n_gate))
    a_scale = jnp.concatenate([jnp.full((aw,), A_QK_DIM ** -0.5 * LOG2E, F32), jnp.ones((2 * aw,), F32)])
    lam_vecs = jnp.stack([lambda_q1, lambda_k1, lambda_q2, lambda_k2]).astype(F32)

    h = _rmsnorm(x, g_pre_mix)
    qk = _proj_conv_silu(h, wt_pre, 2 * mw, m_conv_w, m_conv_b, seq_len)
    v = _proj_act(h, wt_pre, 2 * mw, mw, None, None, "identity", BF16)
    og = _proj_act(h, wt_pre, 3 * mw, mw, None, None, "sigmoid", BF16)
    gates = _proj_act(h, wt_gate, 0, V7X_LANES, jnp.ones((V7X_LANES,), F32), b_gate, "identity", F32)
    a = _proj_act(h, wt_post, 0, 3 * aw, a_scale, jnp.zeros((3 * aw,), F32), "identity", BF16)
    g = _proj_act(h, wt_post, 3 * aw, 2 * D, None, None, "sigmoid", BF16)

    hm = _mlstm(qk, v, og, gates, m_head_norm, batch, seq_len)
    ha = _diff_attention(a, lam_vecs, a_head_norm, lam_init, batch, seq_len)

    mix = _branch_merge(hm, ha, w_branch_m.astype(BF16), w_branch_a.astype(BF16), g)
    x1, h2 = _outproj(mix, w_out.astype(BF16), x, g_post_mix, g_pre_ffn)
    act = _ffn_up(h2, w_up.astype(BF16), ffn_conv_w, ffn_conv_b, seq_len)
    return _ffn_down(act, w_down.astype(BF16), x1, g_post_ffn)


def kernel(x, g_pre_mix, w_in, b_gates, m_conv_w, m_conv_b, m_head_norm, lambda_q1, lambda_k1, lambda_q2, lambda_k2, a_head_norm, w_branch_m, w_branch_a, w_out, g_post_mix, g_pre_ffn, w_up, ffn_conv_w, ffn_conv_b, w_down, g_post_ffn):
    B, S, D = x.shape
    layers = (g_pre_mix, w_in, b_gates, m_conv_w, m_conv_b, m_head_norm, lambda_q1, lambda_k1,
              lambda_q2, lambda_k2, a_head_norm, w_branch_m, w_branch_a, w_out, g_post_mix,
              g_pre_ffn, w_up, ffn_conv_w, ffn_conv_b, w_down, g_post_ffn)
    y = x.reshape(B * S, D)
    for l in range(w_in.shape[0]):
        y = _layer(y, l, *[p[l] for p in layers], batch=B, seq_len=S)
    return y.reshape(B, S, D)
```

```python
import functools
import math

import numpy as np
import jax
import jax.numpy as jnp
from jax import lax
from jax.experimental import pallas as pl
from jax.experimental.pallas import tpu as pltpu

F32 = jnp.float32
BF16 = jnp.bfloat16

EPS = 1e-6
CHUNK = 64
M_HEADS = 8
M_HEAD_DIM = 256
A_HEADS = 8
A_QK_DIM = 128
A_V_DIM = 256
ALIBI_SLOPES = 2.0 ** (-8.0 * np.arange(1, A_HEADS + 1) / A_HEADS)

V7X_LANES = 128
V7X_SUBLANES = 8
V7X_MXU_WIDTH = 256
V7X_NUM_MXU = 2
V7X_SCOPED_VMEM_BYTES = 60000 * 1024
COMPILER_TEMP_BYTES = 16 * 1024 * 1024

MLSTM_CHUNK = V7X_MXU_WIDTH
ATTN_TILE = 512
ATTN_VT_PAD = 16
UNDERFLOW_EXP2 = 136.0
ATTN_BOUND_MARGIN = 1.0
MASK_VALUE = -1e30
LOG2E = math.log2(math.e)


def _vmem_limit(*buffer_bytes):
    return int(min(sum(buffer_bytes) + COMPILER_TEMP_BYTES, V7X_SCOPED_VMEM_BYTES))


def _nbytes(shape, dtype):
    return int(np.prod(shape)) * jnp.dtype(dtype).itemsize


def _params(semantics, *buffer_bytes):
    return pltpu.CompilerParams(dimension_semantics=semantics,
                                vmem_limit_bytes=_vmem_limit(*buffer_bytes))


def _rms_scale(y):
    return lax.rsqrt(jnp.mean(y * y, axis=-1, keepdims=True) + EPS)


def _sigmoid(x):
    return 0.5 * jnp.tanh(0.5 * x) + 0.5


def _col_blocks(width):
    sub = V7X_NUM_MXU * V7X_MXU_WIDTH
    if width % sub:
        return [slice(0, width)]
    return [slice(c, c + sub) for c in range(0, width, sub)]


def _rmsnorm_kernel(x_ref, g_ref, o_ref):
    x = x_ref[...]
    o_ref[...] = (x * _rms_scale(x) * g_ref[...]).astype(o_ref.dtype)


def _rmsnorm(x, g, tm=512):
    T, D = x.shape
    return pl.pallas_call(
        _rmsnorm_kernel,
        grid=(T // tm,),
        in_specs=[pl.BlockSpec((tm, D), lambda i: (i, 0)),
                  pl.BlockSpec((1, D), lambda i: (0, 0))],
        out_specs=pl.BlockSpec((tm, D), lambda i: (i, 0)),
        out_shape=jax.ShapeDtypeStruct((T, D), BF16),
        compiler_params=_params(("parallel",), 2 * _nbytes((tm, D), F32), 2 * _nbytes((tm, D), BF16)),
        name="rmsnorm",
    )(x, g.reshape(1, D))


def _cast_rows_kernel(main_ref, next_ref, o_ref, *, lead):
    if lead:
        rows = jnp.concatenate([main_ref[lead:, :], next_ref[...]], axis=0)
    else:
        rows = main_ref[...]
    o_ref[...] = rows.astype(o_ref.dtype)


def _cast_rows(wt, row0, lead, n_rows, tr=512):
    K = wt.shape[1]
    nxt = max(lead, V7X_SUBLANES)
    assert row0 % tr == 0 and n_rows % tr == 0 and lead % V7X_SUBLANES == 0 and tr % nxt == 0
    i0 = row0 // tr
    return pl.pallas_call(
        functools.partial(_cast_rows_kernel, lead=lead),
        grid=(n_rows // tr,),
        in_specs=[pl.BlockSpec((tr, K), lambda i: (i0 + i, 0)),
                  pl.BlockSpec((nxt, K), lambda i: ((i0 + i + 1) * (tr // nxt), 0))],
        out_specs=pl.BlockSpec((tr, K), lambda i: (i, 0)),
        out_shape=jax.ShapeDtypeStruct((n_rows, K), BF16),
        compiler_params=_params(("parallel",), 2 * _nbytes((tr, K), F32), 2 * _nbytes((tr, K), BF16)),
        name="cast_rows",
    )(wt, wt)


def _load_conv_halo(buf_ref, carry_ref, first_tile):
    halo = V7X_SUBLANES

    @pl.when(first_tile)
    def _():
        buf_ref[0:halo, :] = jnp.zeros((halo, buf_ref.shape[1]), F32)

    @pl.when(jnp.logical_not(first_tile))
    def _():
        buf_ref[0:halo, :] = carry_ref[...]


def _causal_conv_rows(acc, rows, cols, buf_ref, carry_ref, cw_ref, cb_ref, taps):
    halo = V7X_SUBLANES
    r0, r1 = rows.start, rows.stop
    buf_ref[halo + r0:halo + r1, cols] = acc
    if r1 == buf_ref.shape[0] - halo:
        carry_ref[:, cols] = acc[r1 - r0 - halo:, :]
    y = cb_ref[:, cols] + cw_ref[taps - 1:taps, cols] * acc
    window = buf_ref[r0:halo + r1, cols]
    for k in range(taps - 1):
        back = taps - 1 - k
        y = y + cw_ref[k:k + 1, cols] * pltpu.roll(window, back, 0)[halo:]
    return y


def _row_blocks(tm, split):
    step = tm // split
    return [slice(r, r + step) for r in range(0, tm, step)]


def _dot_nt(x, wt):
    return lax.dot_general(x, wt, (((1,), (1,)), ((), ())), preferred_element_type=F32)


def _proj_conv_silu_kernel(h_ref, wt_ref, cw_ref, cb_ref, o_ref, carry_ref, buf_ref, *,
                           tiles_per_seq, taps, row_split):
    i = pl.program_id(0)
    j = pl.program_id(1)
    _load_conv_halo(buf_ref, carry_ref.at[j], (i % tiles_per_seq) == 0)
    for rows in _row_blocks(o_ref.shape[0], row_split):
        for cols in _col_blocks(o_ref.shape[1]):
            acc = _dot_nt(h_ref[rows, :], wt_ref[cols, :])
            y = _causal_conv_rows(acc, rows, cols, buf_ref, carry_ref.at[j], cw_ref, cb_ref, taps)
            o_ref[rows, cols] = (y * _sigmoid(y)).astype(o_ref.dtype)


def _proj_conv_silu(h, wt, n_cols, cw, cb, seq_len, tm=1024, tn=1024, row_split=1):
    T, K = h.shape
    taps = cw.shape[0]
    nj = n_cols // tn
    kern = functools.partial(_proj_conv_silu_kernel, tiles_per_seq=seq_len // tm, taps=taps,
                             row_split=row_split)
    return pl.pallas_call(
        kern,
        grid=(T // tm, nj),
        in_specs=[pl.BlockSpec((tm, K), lambda i, j: (i, 0)),
                  pl.BlockSpec((tn, K), lambda i, j: (j, 0)),
                  pl.BlockSpec((taps, tn), lambda i, j: (0, j)),
                  pl.BlockSpec((1, tn), lambda i, j: (0, j))],
        out_specs=pl.BlockSpec((tm, tn), lambda i, j: (i, j)),
        out_shape=jax.ShapeDtypeStruct((T, n_cols), BF16),
        scratch_shapes=[pltpu.VMEM((nj, V7X_SUBLANES, tn), F32),
                        pltpu.VMEM((tm + V7X_SUBLANES, tn), F32)],
        compiler_params=_params(("arbitrary", "arbitrary"),
                                2 * _nbytes((tm, K), BF16), 2 * _nbytes((K, tn), BF16),
                                2 * _nbytes((tm, tn), BF16), 2 * _nbytes((tm, tn), F32)),
        name="proj_conv_silu",
    )(h, wt, cw, cb.reshape(1, n_cols))


def _proj_act_kernel(h_ref, wt_ref, sc_ref, b_ref, o_ref, *, act, affine):
    for cols in _col_blocks(o_ref.shape[1]):
        acc = _dot_nt(h_ref[...], wt_ref[cols, :])
        if affine:
            acc = acc * sc_ref[:, cols] + b_ref[:, cols]
        if act == "sigmoid":
            acc = _sigmoid(acc)
        o_ref[:, cols] = acc.astype(o_ref.dtype)


def _proj_act(h, wt, row0, n_cols, scale, bias, act, out_dtype, tm=1024, tn=1024):
    T, K = h.shape
    tn = min(tn, n_cols)
    j0 = row0 // tn
    affine = scale is not None
    if not affine:
        scale, bias = jnp.ones((n_cols,), F32), jnp.zeros((n_cols,), F32)
    return pl.pallas_call(
        functools.partial(_proj_act_kernel, act=act, affine=affine),
        grid=(T // tm, n_cols // tn),
        in_specs=[pl.BlockSpec((tm, K), lambda i, j: (i, 0)),
                  pl.BlockSpec((tn, K), lambda i, j: (j0 + j, 0)),
                  pl.BlockSpec((1, tn), lambda i, j: (0, j)),
                  pl.BlockSpec((1, tn), lambda i, j: (0, j))],
        out_specs=pl.BlockSpec((tm, tn), lambda i, j: (i, j)),
        out_shape=jax.ShapeDtypeStruct((T, n_cols), out_dtype),
        compiler_params=_params(("parallel", "parallel"),
                                2 * _nbytes((tm, K), BF16), 2 * _nbytes((K, tn), BF16),
                                2 * _nbytes((tm, tn), out_dtype), _nbytes((tm, tn), F32)),
        name="proj_" + act,
    )(h, wt, scale.reshape(1, n_cols), bias.reshape(1, n_cols))


def _mlstm_kernel(q_ref, k_ref, v_ref, og_ref, g_ref, gain_ref, o_ref, c_ref, n_ref, m_ref, *,
                  heads, head_dim):
    L = q_ref.shape[0]
    H, Dh = heads, head_dim
    q_scale = Dh ** -0.5

    @pl.when(pl.program_id(1) == 0)
    def _():
        c_ref[...] = jnp.zeros_like(c_ref)
        n_ref[...] = jnp.zeros_like(n_ref)
        m_ref[...] = jnp.zeros_like(m_ref)

    gates = g_ref[...]
    logf = jnp.minimum(gates, 0.0) - jnp.log1p(jnp.exp(-jnp.abs(gates)))
    srow = lax.broadcasted_iota(jnp.int32, (L, L), 0)
    tcol = lax.broadcasted_iota(jnp.int32, (L, L), 1)
    causal = srow <= tcol
    cum = jnp.dot((tcol <= srow).astype(F32), logf, precision=lax.Precision.HIGHEST,
                  preferred_element_type=F32)
    u_all = gates - pltpu.roll(cum, V7X_LANES - H, 1)
    u_all_t = u_all.T
    cum_t = cum.T

    for h in range(H):
        hs = slice(h * Dh, (h + 1) * Dh)
        u_row = u_all_t[h:h + 1, :]
        b_row = cum_t[H + h:H + h + 1, :]
        u_keys = jnp.broadcast_to(u_all[:, h:h + 1], (L, L))
        m_prev = m_ref[h, 0:1, 0:1]
        cu = jnp.max(jnp.where(causal, u_keys, -jnp.inf), axis=0, keepdims=True)
        mm = jnp.maximum(m_prev, cu)
        w_intra = jnp.where(causal, jnp.exp(u_keys - mm), 0.0) * q_scale
        w_inter = jnp.exp(m_prev - mm) * q_scale

        qh = q_ref[:, hs]
        kh = k_ref[:, hs]
        vt = v_ref[:, hs].T
        sqk = _dot_nt(kh, qh) * w_intra
        num = (jnp.dot(vt, sqk.astype(BF16), preferred_element_type=F32)
               + w_inter * _dot_nt(c_ref[h].astype(BF16), qh))
        qn = _dot_nt(n_ref[h].astype(BF16), qh)[0:1, :]
        den = jnp.sum(sqk, axis=0, keepdims=True) + w_inter * qn
        hh = num / jnp.maximum(jnp.abs(den), jnp.exp(-(b_row + mm)))
        scale = lax.rsqrt(jnp.mean(hh * hh, axis=0, keepdims=True) + EPS)
        y = (hh * scale).T * gain_ref[:, hs]
        o_ref[:, hs] = (y * og_ref[:, hs].astype(F32)).astype(o_ref.dtype)

        mm_last = mm[:, L - 1:L]
        ws = jnp.exp(u_row - mm_last)
        decay = jnp.exp(m_prev - mm_last)
        c_ref[h] = decay * c_ref[h] + jnp.dot((vt.astype(F32) * ws).astype(BF16), kh,
                                              preferred_element_type=F32)
        ws_rows = jnp.broadcast_to(ws, (V7X_SUBLANES, L)).astype(BF16)
        n_ref[h] = decay * n_ref[h] + jnp.dot(ws_rows, kh, preferred_element_type=F32)
        m_ref[h] = jnp.broadcast_to(b_row[:, L - 1:L] + mm_last, m_ref.shape[1:])


def _mlstm(qk, v, og, gates, gain, batch, seq_len):
    T = qk.shape[0]
    H, Dh, L = M_HEADS, M_HEAD_DIM, MLSTM_CHUNK
    W = H * Dh
    nt = seq_len // L
    row_blk = lambda b, t: (b * nt + t, 0)
    return pl.pallas_call(
        functools.partial(_mlstm_kernel, heads=H, head_dim=Dh),
        grid=(batch, nt),
        in_specs=[pl.BlockSpec((L, W), row_blk),
                  pl.BlockSpec((L, W), lambda b, t: (b * nt + t, 1)),
                  pl.BlockSpec((L, W), row_blk),
                  pl.BlockSpec((L, W), row_blk),
                  pl.BlockSpec((L, V7X_LANES), row_blk),
                  pl.BlockSpec((1, W), lambda b, t: (0, 0))],
        out_specs=pl.BlockSpec((L, W), row_blk),
        out_shape=jax.ShapeDtypeStruct((T, W), BF16),
        scratch_shapes=[pltpu.VMEM((H, Dh, Dh), F32),
                        pltpu.VMEM((H, V7X_SUBLANES, Dh), F32),
                        pltpu.VMEM((H, V7X_SUBLANES, V7X_LANES), F32)],
        compiler_params=_params(("arbitrary", "arbitrary"),
                                10 * _nbytes((L, W), BF16), _nbytes((H, Dh, Dh), F32)),
        name="mlstm",
    )(qk, qk, v, og, gates, gain.reshape(1, W))


def _attn_kernel(slopes_ref, lam_ref, q_ref, k_ref, v_ref, gain_ref, o_ref,
                 vt_ref, acc_ref, m_ref, s_ref, p_ref, al_ref, kmax_ref, bias_ref, *, tile, lam_init):
    h = pl.program_id(1)
    qi = pl.program_id(2)
    dk = A_QK_DIM
    dv = A_V_DIM
    slope = slopes_ref[h] * LOG2E
    seq = k_ref.shape[0]

    def max_row_norm_sq(x):
        sq = x.astype(F32)
        sq = sq * sq
        return jnp.maximum(jnp.max(jnp.sum(sq[:, 0:dk], axis=1, keepdims=True)),
                           jnp.max(jnp.sum(sq[:, dk:2 * dk], axis=1, keepdims=True)))

    @pl.when(qi == 0)
    def _():
        kmax = jnp.float32(0.0)
        for c in range(seq // tile):
            rows = slice(c * tile, (c + 1) * tile)
            vt_ref[0:dv, rows] = v_ref[rows, :].T
            kmax = jnp.maximum(kmax, max_row_norm_sq(k_ref[rows, :]))
        extra = lax.broadcasted_iota(jnp.int32, (ATTN_VT_PAD, seq), 0)
        vt_ref[dv:dv + ATTN_VT_PAD, :] = jnp.where(extra == 0, 1.0, 0.0).astype(BF16)
        kmax_ref[0] = kmax
        krow = lax.broadcasted_iota(jnp.int32, (tile, tile), 0)
        qcol = lax.broadcasted_iota(jnp.int32, (tile, tile), 1)
        rel = (qcol - krow).astype(F32)
        bias_ref[0] = -slope * rel
        allowed = (krow // CHUNK) <= (qcol // CHUNK)
        bias_ref[1] = jnp.where(allowed, -slope * jnp.abs(rel), MASK_VALUE)

    m_ref[...] = jnp.full(m_ref.shape, MASK_VALUE, F32)
    acc_ref[...] = jnp.zeros_like(acc_ref)
    bias_full = bias_ref.at[0]

    def scores(j, slot):
        kt = k_ref[pl.ds(pl.multiple_of(j * tile, tile), tile), :]
        for c in range(2):
            s_ref[slot, c] = lax.dot_general(kt[:, c * dk:(c + 1) * dk], q_ref[:, c * dk:(c + 1) * dk],
                                             (((1,), (1,)), ((), ())), preferred_element_type=F32)

    def softmax_step(slot, bias, shift):
        for c in range(2):
            m_prev = m_ref[c]
            m_new = jnp.maximum(m_prev, jnp.max(s_ref[slot, c] + bias[...], axis=0, keepdims=True) + shift)
            m_ref[c] = m_new
            p_ref[slot, c] = jnp.exp2((bias[...] - (m_new - shift)) + s_ref[slot, c]).astype(BF16)
            al_ref[slot, c] = jnp.exp2(m_prev - m_new)

    def accumulate(j, slot):
        vt = vt_ref[:, pl.ds(pl.multiple_of(j * tile, tile), tile)]
        for c in range(2):
            acc_ref[c] = al_ref[slot, c] * acc_ref[c] + jnp.dot(vt, p_ref[slot, c], preferred_element_type=F32)

    def tile_of(t):
        return jnp.clip(qi - 1 - t, 0, qi)

    def shift_of(t, n_valid):
        return jnp.where(t < n_valid, -slope * jnp.asarray((t + 1) * tile).astype(F32), MASK_VALUE)

    scores(qi, 0)
    scores(tile_of(0), 1)
    softmax_step(0, bias_ref.at[1], 0.0)
    softmax_step(1, bias_full, shift_of(0, qi))

    qk_bound = jnp.sqrt(max_row_norm_sq(q_ref[...]) * kmax_ref[0]) + ATTN_BOUND_MARGIN
    reach = (qk_bound + UNDERFLOW_EXP2 - jnp.min(m_ref[...])) / (slope * tile)
    n_keep = jnp.minimum(qi, jnp.clip(reach, 0.0, 1e6).astype(jnp.int32) + 1)
    pairs = lax.div(jnp.maximum(n_keep - 1, 0) + 1, 2)

    @pl.when(pairs > 0)
    def _():
        scores(tile_of(1), 0)
        scores(tile_of(2), 1)

    def stage(a, lookahead):
        accumulate(tile_of(a - 2), 0)
        accumulate(tile_of(a - 1), 1)
        softmax_step(0, bias_full, shift_of(a, n_keep))
        if lookahead:
            scores(tile_of(a + 2), 0)
        softmax_step(1, bias_full, shift_of(a + 1, n_keep))
        if lookahead:
            scores(tile_of(a + 3), 1)

    def pair(g, carry):
        stage(1 + 2 * g, True)
        return carry

    lax.fori_loop(0, pairs - 1, pair, 0)

    @pl.when(pairs > 0)
    def _():
        stage(2 * pairs - 1, False)

    last = 2 * pairs + 1
    accumulate(tile_of(last - 2), 0)
    accumulate(tile_of(last - 1), 1)

    lam_vec = lam_ref[...]
    lam = (jnp.exp(jnp.sum(lam_vec[0:1] * lam_vec[1:2], axis=1, keepdims=True))
           - jnp.exp(jnp.sum(lam_vec[2:3] * lam_vec[3:4], axis=1, keepdims=True)) + lam_init)
    o = (acc_ref[0, 0:dv] / acc_ref[0, dv:dv + 1]
         - lam * (acc_ref[1, 0:dv] / acc_ref[1, dv:dv + 1]))
    scale = lax.rsqrt(jnp.mean(o * o, axis=0, keepdims=True) + EPS) * (1.0 - lam_init)
    o_ref[...] = ((o * scale).T * gain_ref[...]).astype(o_ref.dtype)


def _diff_attention(a, lam_vecs, gain, lam_init, batch, seq_len):
    T = a.shape[0]
    H, Dv = A_HEADS, A_V_DIM
    W = H * Dv
    tile = ATTN_TILE
    assert tile % CHUNK == 0 and seq_len % tile == 0
    nq = seq_len // tile
    slopes = jnp.asarray(ALIBI_SLOPES, F32)
    return pl.pallas_call(
        functools.partial(_attn_kernel, tile=tile, lam_init=lam_init),
        grid=(batch, H, nq),
        in_specs=[pl.BlockSpec(memory_space=pltpu.SMEM),
                  pl.BlockSpec((4, A_QK_DIM), lambda b, h, i: (0, 0)),
                  pl.BlockSpec((tile, Dv), lambda b, h, i: (b * nq + i, h)),
                  pl.BlockSpec((seq_len, Dv), lambda b, h, i: (b, H + h)),
                  pl.BlockSpec((seq_len, Dv), lambda b, h, i: (b, 2 * H + h)),
                  pl.BlockSpec((1, Dv), lambda b, h, i: (0, h))],
        out_specs=pl.BlockSpec((tile, Dv), lambda b, h, i: (b * nq + i, h)),
        out_shape=jax.ShapeDtypeStruct((T, W), BF16),
        scratch_shapes=[pltpu.VMEM((Dv + ATTN_VT_PAD, seq_len), BF16),
                        pltpu.VMEM((2, Dv + ATTN_VT_PAD, tile), F32),
                        pltpu.VMEM((2, 1, tile), F32),
                        pltpu.VMEM((2, 2, tile, tile), F32),
                        pltpu.VMEM((2, 2, tile, tile), BF16),
                        pltpu.VMEM((2, 2, 1, tile), F32),
                        pltpu.SMEM((1,), F32),
                        pltpu.VMEM((2, tile, tile), F32)],
        compiler_params=_params(("arbitrary", "arbitrary", "arbitrary"),
                                5 * _nbytes((seq_len, Dv), BF16), 4 * _nbytes((tile, Dv), BF16),
                                2 * _nbytes((Dv, tile), F32), 4 * _nbytes((tile, tile), F32),
                                4 * _nbytes((tile, tile), BF16)),
        name="diff_attention",
    )(slopes, lam_vecs, a, a, a, gain.reshape(1, W))


def _merge_kernel(am_ref, aa_ref, wm_ref, wa_ref, gm_ref, ga_ref, o_ref):
    for cols in _col_blocks(o_ref.shape[1]):
        ym = jnp.dot(am_ref[...], wm_ref[:, cols], preferred_element_type=F32)
        ya = jnp.dot(aa_ref[...], wa_ref[:, cols], preferred_element_type=F32)
        o_ref[:, cols] = (gm_ref[:, cols].astype(F32) * ym + ga_ref[:, cols].astype(F32) * ya).astype(o_ref.dtype)


def _branch_merge(hm, ha, wm, wa, g, tm=1024, tn=1024):
    T, K = hm.shape
    N = wm.shape[1]
    nj = N // tn
    return pl.pallas_call(
        _merge_kernel,
        grid=(T // tm, nj),
        in_specs=[pl.BlockSpec((tm, K), lambda i, j: (i, 0)),
                  pl.BlockSpec((tm, K), lambda i, j: (i, 0)),
                  pl.BlockSpec((K, tn), lambda i, j: (0, j)),
                  pl.BlockSpec((K, tn), lambda i, j: (0, j)),
                  pl.BlockSpec((tm, tn), lambda i, j: (i, j)),
                  pl.BlockSpec((tm, tn), lambda i, j: (i, nj + j))],
        out_specs=pl.BlockSpec((tm, tn), lambda i, j: (i, j)),
        out_shape=jax.ShapeDtypeStruct((T, N), BF16),
        compiler_params=_params(("parallel", "parallel"),
                                4 * _nbytes((tm, K), BF16), 4 * _nbytes((K, tn), BF16),
                                6 * _nbytes((tm, tn), BF16), 2 * _nbytes((tm, tn), F32)),
        name="branch_merge",
    )(hm, ha, wm, wa, g, g)


def _outproj_kernel(mix_ref, w_ref, x_ref, gpost_ref, gpre_ref, x1_ref, h2_ref, *, row_split):
    for rows in _row_blocks(x1_ref.shape[0], row_split):
        y = jnp.dot(mix_ref[rows, :], w_ref[...], preferred_element_type=F32)
        x1 = x_ref[rows, :] + y * _rms_scale(y) * gpost_ref[...]
        x1_ref[rows, :] = x1
        h2_ref[rows, :] = (x1 * _rms_scale(x1) * gpre_ref[...]).astype(h2_ref.dtype)


def _outproj(mix, w, x, g_post, g_pre, tm=512, row_split=4):
    T, K = mix.shape
    D = w.shape[1]
    row = lambda i: (i, 0)
    fixed = lambda i: (0, 0)
    return pl.pallas_call(
        functools.partial(_outproj_kernel, row_split=row_split),
        grid=(T // tm,),
        in_specs=[pl.BlockSpec((tm, K), row),
                  pl.BlockSpec((K, D), fixed),
                  pl.BlockSpec((tm, D), row),
                  pl.BlockSpec((1, D), fixed),
                  pl.BlockSpec((1, D), fixed)],
        out_specs=[pl.BlockSpec((tm, D), row), pl.BlockSpec((tm, D), row)],
        out_shape=[jax.ShapeDtypeStruct((T, D), F32), jax.ShapeDtypeStruct((T, D), BF16)],
        compiler_params=_params(("parallel",),
                                2 * _nbytes((tm, K), BF16), 2 * _nbytes((K, D), BF16),
                                4 * _nbytes((tm, D), F32), 2 * _nbytes((tm, D), BF16),
                                2 * _nbytes((tm, D), F32)),
        name="outproj_norm",
    )(mix, w, x, g_post.reshape(1, D), g_pre.reshape(1, D))


def _gelu_tanh(x):
    return 0.5 * x * (1.0 + jnp.tanh(math.sqrt(2.0 / math.pi) * (x + 0.044715 * (x * x * x))))


def _ffn_up_kernel(h_ref, wg_ref, wv_ref, cwg_ref, cwv_ref, cbg_ref, cbv_ref, o_ref,
                   carry_ref, buf_ref, *, tiles_per_seq, taps, row_split):
    i = pl.program_id(0)
    j = pl.program_id(1)
    first = (i % tiles_per_seq) == 0
    _load_conv_halo(buf_ref.at[0], carry_ref.at[j, 0], first)
    _load_conv_halo(buf_ref.at[1], carry_ref.at[j, 1], first)
    for rows in _row_blocks(o_ref.shape[0], row_split):
        for cols in _col_blocks(o_ref.shape[1]):
            h = h_ref[rows, :]
            gate = _causal_conv_rows(jnp.dot(h, wg_ref[:, cols], preferred_element_type=F32), rows, cols,
                                     buf_ref.at[0], carry_ref.at[j, 0], cwg_ref, cbg_ref, taps)
            val = _causal_conv_rows(jnp.dot(h, wv_ref[:, cols], preferred_element_type=F32), rows, cols,
                                    buf_ref.at[1], carry_ref.at[j, 1], cwv_ref, cbv_ref, taps)
            o_ref[rows, cols] = (_gelu_tanh(gate) * val).astype(o_ref.dtype)


def _ffn_up(h2, w_up, cw, cb, seq_len, tm=1024, tf=512, row_split=1):
    T, K = h2.shape
    F = w_up.shape[1] // 2
    taps = cw.shape[0]
    nj = F // tf
    gate_col = lambda i, j: (0, j)
    val_col = lambda i, j: (0, nj + j)
    cb2 = cb.reshape(1, 2 * F)
    return pl.pallas_call(
        functools.partial(_ffn_up_kernel, tiles_per_seq=seq_len // tm, taps=taps, row_split=row_split),
        grid=(T // tm, nj),
        in_specs=[pl.BlockSpec((tm, K), lambda i, j: (i, 0)),
                  pl.BlockSpec((K, tf), gate_col),
                  pl.BlockSpec((K, tf), val_col),
                  pl.BlockSpec((taps, tf), gate_col),
                  pl.BlockSpec((taps, tf), val_col),
                  pl.BlockSpec((1, tf), gate_col),
                  pl.BlockSpec((1, tf), val_col)],
        out_specs=pl.BlockSpec((tm, tf), lambda i, j: (i, j)),
        out_shape=jax.ShapeDtypeStruct((T, F), BF16),
        scratch_shapes=[pltpu.VMEM((nj, 2, V7X_SUBLANES, tf), F32),
                        pltpu.VMEM((2, tm + V7X_SUBLANES, tf), F32)],
        compiler_params=_params(("arbitrary", "arbitrary"),
                                2 * _nbytes((tm, K), BF16), 4 * _nbytes((K, tf), BF16),
                                2 * _nbytes((tm, tf), BF16), 4 * _nbytes((tm, tf), F32)),
        name="ffn_up_conv_gelu",
    )(h2, w_up, w_up, cw, cw, cb2, cb2)


def _ffn_down_kernel(a_ref, w_ref, x1_ref, g_ref, o_ref, *, row_split):
    for rows in _row_blocks(o_ref.shape[0], row_split):
        y = jnp.dot(a_ref[rows, :], w_ref[...], preferred_element_type=F32)
        o_ref[rows, :] = x1_ref[rows, :] + y * _rms_scale(y) * g_ref[...]


def _ffn_down(act, w, x1, g, tm=512, row_split=4):
    T, F = act.shape
    D = w.shape[1]
    return pl.pallas_call(
        functools.partial(_ffn_down_kernel, row_split=row_split),
        grid=(T // tm,),
        in_specs=[pl.BlockSpec((tm, F), lambda i: (i, 0)),
                  pl.BlockSpec((F, D), lambda i: (0, 0), pipeline_mode=pl.Buffered(1)),
                  pl.BlockSpec((tm, D), lambda i: (i, 0)),
                  pl.BlockSpec((1, D), lambda i: (0, 0))],
        out_specs=pl.BlockSpec((tm, D), lambda i: (i, 0)),
        out_shape=jax.ShapeDtypeStruct((T, D), F32),
        compiler_params=_params(("parallel",),
                                2 * _nbytes((tm, F), BF16), _nbytes((F, D), BF16),
                                4 * _nbytes((tm, D), F32)),
        name="ffn_down_norm",
    )(act, w, x1, g.reshape(1, D))


def _layer(x, l, g_pre_mix, w_in, b_gates, m_conv_w, m_conv_b, m_head_norm,
           lambda_q1, lambda_k1, lambda_q2, lambda_k2, a_head_norm,
           w_branch_m, w_branch_a, w_out, g_post_mix, g_pre_ffn,
           w_up, ffn_conv_w, ffn_conv_b, w_down, g_post_ffn, batch, seq_len):
    D = x.shape[1]
    mw = M_HEADS * M_HEAD_DIM
    aw = A_HEADS * A_V_DIM
    n_gate = 2 * M_HEADS
    pre = 4 * mw
    post0 = pre + n_gate
    lam_init = 0.8 - 0.6 * math.exp(-0.3 * l)

    wt = w_in.T
    wt_pre = _cast_rows(wt, 0, 0, pre)
    wt_post = _cast_rows(wt, pre, n_gate, 3 * aw + 2 * D)
    wt_gate = jnp.pad(wt[pre:post0], ((0, V7X_LANES - n_gate), (0, 0))).astype(BF16)
    b_gate = jnp.pad(b_gates.astype(F32), (0, V7X_LANES - n_gate))
    a_scale = jnp.concatenate([jnp.full((aw,), A_QK_DIM ** -0.5 * LOG2E, F32), jnp.ones((2 * aw,), F32)])
    lam_vecs = jnp.stack([lambda_q1, lambda_k1, lambda_q2, lambda_k2]).astype(F32)

    h = _rmsnorm(x, g_pre_mix)
    qk = _proj_conv_silu(h, wt_pre, 2 * mw, m_conv_w, m_conv_b, seq_len)
    v = _proj_act(h, wt_pre, 2 * mw, mw, None, None, "identity", BF16)
    og = _proj_act(h, wt_pre, 3 * mw, mw, None, None, "sigmoid", BF16)
    gates = _proj_act(h, wt_gate, 0, V7X_LANES, jnp.ones((V7X_LANES,), F32), b_gate, "identity", F32)
    a = _proj_act(h, wt_post, 0, 3 * aw, a_scale, jnp.zeros((3 * aw,), F32), "identity", BF16)
    g = _proj_act(h, wt_post, 3 * aw, 2 * D, None, None, "sigmoid", BF16)

    hm = _mlstm(qk, v, og, gates, m_head_norm, batch, seq_len)
    ha = _diff_attention(a, lam_vecs, a_head_norm, lam_init, batch, seq_len)

    mix = _branch_merge(hm, ha, w_branch_m.astype(BF16), w_branch_a.astype(BF16), g)
    x1, h2 = _outproj(mix, w_out.astype(BF16), x, g_post_mix, g_pre_ffn)
    act = _ffn_up(h2, w_up.astype(BF16), ffn_conv_w, ffn_conv_b, seq_len)
    return _ffn_down(act, w_down.astype(BF16), x1, g_post_ffn)


def kernel(x, g_pre_mix, w_in, b_gates, m_conv_w, m_conv_b, m_head_norm, lambda_q1, lambda_k1, lambda_q2, lambda_k2, a_head_norm, w_branch_m, w_branch_a, w_out, g_post_mix, g_pre_ffn, w_up, ffn_conv_w, ffn_conv_b, w_down, g_post_ffn):
    B, S, D = x.shape
    layers = (g_pre_mix, w_in, b_gates, m_conv_w, m_conv_b, m_head_norm, lambda_q1, lambda_k1,
              lambda_q2, lambda_k2, a_head_norm, w_branch_m, w_branch_a, w_out, g_post_mix,
              g_pre_ffn, w_up, ffn_conv_w, ffn_conv_b, w_down, g_post_ffn)
    y = x.reshape(B * S, D)
    for l in range(w_in.shape[0]):
        y = _layer(y, l, *[p[l] for p in layers], batch=B, seq_len=S)
    return y.reshape(B, S, D)
```

```python
import functools
import math

import numpy as np
import jax
import jax.numpy as jnp
from jax import lax
from jax.experimental import pallas as pl
from jax.experimental.pallas import tpu as pltpu

F32 = jnp.float32
BF16 = jnp.bfloat16

EPS = 1e-6
CHUNK = 64
M_HEADS = 8
M_HEAD_DIM = 256
A_HEADS = 8
A_QK_DIM = 128
A_V_DIM = 256
ALIBI_SLOPES = 2.0 ** (-8.0 * np.arange(1, A_HEADS + 1) / A_HEADS)

V7X_LANES = 128
V7X_SUBLANES = 8
V7X_MXU_WIDTH = 256
V7X_NUM_MXU = 2
V7X_SCOPED_VMEM_BYTES = 60000 * 1024
COMPILER_TEMP_BYTES = 16 * 1024 * 1024

MLSTM_CHUNK = V7X_MXU_WIDTH
ATTN_TILE = 512
ATTN_VT_PAD = 16
UNDERFLOW_EXP2 = 136.0
ATTN_BOUND_MARGIN = 1.0
MASK_VALUE = -1e30
LOG2E = math.log2(math.e)


def _vmem_limit(*buffer_bytes):
    return int(min(sum(buffer_bytes) + COMPILER_TEMP_BYTES, V7X_SCOPED_VMEM_BYTES))


def _nbytes(shape, dtype):
    return int(np.prod(shape)) * jnp.dtype(dtype).itemsize


def _params(semantics, *buffer_bytes):
    return pltpu.CompilerParams(dimension_semantics=semantics,
                                vmem_limit_bytes=_vmem_limit(*buffer_bytes))


def _rms_scale(y):
    return lax.rsqrt(jnp.mean(y * y, axis=-1, keepdims=True) + EPS)


def _sigmoid(x):
    return 0.5 * jnp.tanh(0.5 * x) + 0.5


def _col_blocks(width):
    sub = V7X_NUM_MXU * V7X_MXU_WIDTH
    if width % sub:
        return [slice(0, width)]
    return [slice(c, c + sub) for c in range(0, width, sub)]


def _rmsnorm_kernel(x_ref, g_ref, o_ref):
    x = x_ref[...]
    o_ref[...] = (x * _rms_scale(x) * g_ref[...]).astype(o_ref.dtype)


def _rmsnorm(x, g, tm=512):
    T, D = x.shape
    return pl.pallas_call(
        _rmsnorm_kernel,
        grid=(T // tm,),
        in_specs=[pl.BlockSpec((tm, D), lambda i: (i, 0)),
                  pl.BlockSpec((1, D), lambda i: (0, 0))],
        out_specs=pl.BlockSpec((tm, D), lambda i: (i, 0)),
        out_shape=jax.ShapeDtypeStruct((T, D), BF16),
        compiler_params=_params(("parallel",), 2 * _nbytes((tm, D), F32), 2 * _nbytes((tm, D), BF16)),
        name="rmsnorm",
    )(x, g.reshape(1, D))


def _cast_rows_kernel(main_ref, next_ref, o_ref, *, lead):
    if lead:
        rows = jnp.concatenate([main_ref[lead:, :], next_ref[...]], axis=0)
    else:
        rows = main_ref[...]
    o_ref[...] = rows.astype(o_ref.dtype)


def _cast_rows(wt, row0, lead, n_rows, tr=512):
    K = wt.shape[1]
    nxt = max(lead, V7X_SUBLANES)
    assert row0 % tr == 0 and n_rows % tr == 0 and lead % V7X_SUBLANES == 0 and tr % nxt == 0
    i0 = row0 // tr
    return pl.pallas_call(
        functools.partial(_cast_rows_kernel, lead=lead),
        grid=(n_rows // tr,),
        in_specs=[pl.BlockSpec((tr, K), lambda i: (i0 + i, 0)),
                  pl.BlockSpec((nxt, K), lambda i: ((i0 + i + 1) * (tr // nxt), 0))],
        out_specs=pl.BlockSpec((tr, K), lambda i: (i, 0)),
        out_shape=jax.ShapeDtypeStruct((n_rows, K), BF16),
        compiler_params=_params(("parallel",), 2 * _nbytes((tr, K), F32), 2 * _nbytes((tr, K), BF16)),
        name="cast_rows",
    )(wt, wt)


def _load_conv_halo(buf_ref, carry_ref, first_tile):
    halo = V7X_SUBLANES

    @pl.when(first_tile)
    def _():
        buf_ref[0:halo, :] = jnp.zeros((halo, buf_ref.shape[1]), F32)

    @pl.when(jnp.logical_not(first_tile))
    def _():
        buf_ref[0:halo, :] = carry_ref[...]


def _causal_conv_rows(acc, rows, cols, buf_ref, carry_ref, cw_ref, cb_ref, taps):
    halo = V7X_SUBLANES
    r0, r1 = rows.start, rows.stop
    buf_ref[halo + r0:halo + r1, cols] = acc
    if r1 == buf_ref.shape[0] - halo:
        carry_ref[:, cols] = acc[r1 - r0 - halo:, :]
    y = cb_ref[:, cols] + cw_ref[taps - 1:taps, cols] * acc
    window = buf_ref[r0:halo + r1, cols]
    for k in range(taps - 1):
        back = taps - 1 - k
        y = y + cw_ref[k:k + 1, cols] * pltpu.roll(window, back, 0)[halo:]
    return y


def _row_blocks(tm, split):
    step = tm // split
    return [slice(r, r + step) for r in range(0, tm, step)]


def _dot_nt(x, wt):
    return lax.dot_general(x, wt, (((1,), (1,)), ((), ())), preferred_element_type=F32)


def _proj_conv_silu_kernel(h_ref, wt_ref, cw_ref, cb_ref, o_ref, carry_ref, buf_ref, *,
                           tiles_per_seq, taps, row_split):
    i = pl.program_id(0)
    j = pl.program_id(1)
    _load_conv_halo(buf_ref, carry_ref.at[j], (i % tiles_per_seq) == 0)
    for rows in _row_blocks(o_ref.shape[0], row_split):
        for cols in _col_blocks(o_ref.shape[1]):
            acc = _dot_nt(h_ref[rows, :], wt_ref[cols, :])
            y = _causal_conv_rows(acc, rows, cols, buf_ref, carry_ref.at[j], cw_ref, cb_ref, taps)
            o_ref[rows, cols] = (y * _sigmoid(y)).astype(o_ref.dtype)


def _proj_conv_silu(h, wt, n_cols, cw, cb, seq_len, tm=1024, tn=1024, row_split=1):
    T, K = h.shape
    taps = cw.shape[0]
    nj = n_cols // tn
    kern = functools.partial(_proj_conv_silu_kernel, tiles_per_seq=seq_len // tm, taps=taps,
                             row_split=row_split)
    return pl.pallas_call(
        kern,
        grid=(T // tm, nj),
        in_specs=[pl.BlockSpec((tm, K), lambda i, j: (i, 0)),
                  pl.BlockSpec((tn, K), lambda i, j: (j, 0)),
                  pl.BlockSpec((taps, tn), lambda i, j: (0, j)),
                  pl.BlockSpec((1, tn), lambda i, j: (0, j))],
        out_specs=pl.BlockSpec((tm, tn), lambda i, j: (i, j)),
        out_shape=jax.ShapeDtypeStruct((T, n_cols), BF16),
        scratch_shapes=[pltpu.VMEM((nj, V7X_SUBLANES, tn), F32),
                        pltpu.VMEM((tm + V7X_SUBLANES, tn), F32)],
        compiler_params=_params(("arbitrary", "arbitrary"),
                                2 * _nbytes((tm, K), BF16), 2 * _nbytes((K, tn), BF16),
                                2 * _nbytes((tm, tn), BF16), 2 * _nbytes((tm, tn), F32)),
        name="proj_conv_silu",
    )(h, wt, cw, cb.reshape(1, n_cols))


def _proj_act_kernel(h_ref, wt_ref, sc_ref, b_ref, o_ref, *, act, affine):
    for cols in _col_blocks(o_ref.shape[1]):
        acc = _dot_nt(h_ref[...], wt_ref[cols, :])
        if affine:
            acc = acc * sc_ref[:, cols] + b_ref[:, cols]
        if act == "sigmoid":
            acc = _sigmoid(acc)
        o_ref[:, cols] = acc.astype(o_ref.dtype)


def _proj_act(h, wt, row0, n_cols, scale, bias, act, out_dtype, tm=1024, tn=1024):
    T, K = h.shape
    tn = min(tn, n_cols)
    j0 = row0 // tn
    affine = scale is not None
    if not affine:
        scale, bias = jnp.ones((n_cols,), F32), jnp.zeros((n_cols,), F32)
    return pl.pallas_call(
        functools.partial(_proj_act_kernel, act=act, affine=affine),
        grid=(T // tm, n_cols // tn),
        in_specs=[pl.BlockSpec((tm, K), lambda i, j: (i, 0)),
                  pl.BlockSpec((tn, K), lambda i, j: (j0 + j, 0)),
                  pl.BlockSpec((1, tn), lambda i, j: (0, j)),
                  pl.BlockSpec((1, tn), lambda i, j: (0, j))],
        out_specs=pl.BlockSpec((tm, tn), lambda i, j: (i, j)),
        out_shape=jax.ShapeDtypeStruct((T, n_cols), out_dtype),
        compiler_params=_params(("parallel", "parallel"),
                                2 * _nbytes((tm, K), BF16), 2 * _nbytes((K, tn), BF16),
                                2 * _nbytes((tm, tn), out_dtype), _nbytes((tm, tn), F32)),
        name="proj_" + act,
    )(h, wt, scale.reshape(1, n_cols), bias.reshape(1, n_cols))


def _mlstm_kernel(q_ref, k_ref, v_ref, og_ref, g_ref, gain_ref, o_ref, c_ref, n_ref, m_ref, *,
                  heads, head_dim):
    L = q_ref.shape[0]
    H, Dh = heads, head_dim
    q_scale = Dh ** -0.5

    @pl.when(pl.program_id(1) == 0)
    def _():
        c_ref[...] = jnp.zeros_like(c_ref)
        n_ref[...] = jnp.zeros_like(n_ref)
        m_ref[...] = jnp.zeros_like(m_ref)

    gates = g_ref[...]
    logf = jnp.minimum(gates, 0.0) - jnp.log1p(jnp.exp(-jnp.abs(gates)))
    srow = lax.broadcasted_iota(jnp.int32, (L, L), 0)
    tcol = lax.broadcasted_iota(jnp.int32, (L, L), 1)
    causal = srow <= tcol
    cum = jnp.dot((tcol <= srow).astype(F32), logf, precision=lax.Precision.HIGHEST,
                  preferred_element_type=F32)
    u_all = gates - pltpu.roll(cum, V7X_LANES - H, 1)
    u_all_t = u_all.T
    cum_t = cum.T

    for h in range(H):
        hs = slice(h * Dh, (h + 1) * Dh)
        u_row = u_all_t[h:h + 1, :]
        b_row = cum_t[H + h:H + h + 1, :]
        u_keys = jnp.broadcast_to(u_all[:, h:h + 1], (L, L))
        m_prev = m_ref[h, 0:1, 0:1]
        cu = jnp.max(jnp.where(causal, u_keys, -jnp.inf), axis=0, keepdims=True)
        mm = jnp.maximum(m_prev, cu)
        w_intra = jnp.where(causal, jnp.exp(u_keys - mm), 0.0) * q_scale
        w_inter = jnp.exp(m_prev - mm) * q_scale

        qh = q_ref[:, hs]
        kh = k_ref[:, hs]
        vt = v_ref[:, hs].T
        sqk = _dot_nt(kh, qh) * w_intra
        num = (jnp.dot(vt, sqk.astype(BF16), preferred_element_type=F32)
               + w_inter * _dot_nt(c_ref[h].astype(BF16), qh))
        qn = _dot_nt(n_ref[h].astype(BF16), qh)[0:1, :]
        den = jnp.sum(sqk, axis=0, keepdims=True) + w_inter * qn
        hh = num / jnp.maximum(jnp.abs(den), jnp.exp(-(b_row + mm)))
        scale = lax.rsqrt(jnp.mean(hh * hh, axis=0, keepdims=True) + EPS)
        y = (hh * scale).T * gain_ref[:, hs]
        o_ref[:, hs] = (y * og_ref[:, hs].astype(F32)).astype(o_ref.dtype)

        mm_last = mm[:, L - 1:L]
        ws = jnp.exp(u_row - mm_last)
        decay = jnp.exp(m_prev - mm_last)
        c_ref[h] = decay * c_ref[h] + jnp.dot((vt.astype(F32) * ws).astype(BF16), kh,
                                              preferred_element_type=F32)
        ws_rows = jnp.broadcast_to(ws, (V7X_SUBLANES, L)).astype(BF16)
        n_ref[h] = decay * n_ref[h] + jnp.dot(ws_rows, kh, preferred_element_type=F32)
        m_ref[h] = jnp.broadcast_to(b_row[:, L - 1:L] + mm_last, m_ref.shape[1:])


def _mlstm(qk, v, og, gates, gain, batch, seq_len):
    T = qk.shape[0]
    H, Dh, L = M_HEADS, M_HEAD_DIM, MLSTM_CHUNK
    W = H * Dh
    nt = seq_len // L
    row_blk = lambda b, t: (b * nt + t, 0)
    return pl.pallas_call(
        functools.partial(_mlstm_kernel, heads=H, head_dim=Dh),
        grid=(batch, nt),
        in_specs=[pl.BlockSpec((L, W), row_blk),
                  pl.BlockSpec((L, W), lambda b, t: (b * nt + t, 1)),
                  pl.BlockSpec((L, W), row_blk),
                  pl.BlockSpec((L, W), row_blk),
                  pl.BlockSpec((L, V7X_LANES), row_blk),
                  pl.BlockSpec((1, W), lambda b, t: (0, 0))],
        out_specs=pl.BlockSpec((L, W), row_blk),
        out_shape=jax.ShapeDtypeStruct((T, W), BF16),
        scratch_shapes=[pltpu.VMEM((H, Dh, Dh), F32),
                        pltpu.VMEM((H, V7X_SUBLANES, Dh), F32),
                        pltpu.VMEM((H, V7X_SUBLANES, V7X_LANES), F32)],
        compiler_params=_params(("arbitrary", "arbitrary"),
                                10 * _nbytes((L, W), BF16), _nbytes((H, Dh, Dh), F32)),
        name="mlstm",
    )(qk, qk, v, og, gates, gain.reshape(1, W))


def _max_half_row_norm_sq(x, dk):
    sq = x.astype(F32)
    sq = sq * sq
    return jnp.maximum(jnp.max(jnp.sum(sq[:, 0:dk], axis=1, keepdims=True)),
                       jnp.max(jnp.sum(sq[:, dk:2 * dk], axis=1, keepdims=True)))


def _attn_kernel(slopes_ref, lam_ref, q_ref, k_ref, v_ref, gain_ref, o_ref,
                 vt_ref, acc_ref, m_ref, s_ref, p_ref, al_ref, kmax_ref, bias_ref, *, tile, lam_init):
    dk = A_QK_DIM
    dv = A_V_DIM
    slope = slopes_ref[pl.program_id(1)] * LOG2E
    seq = k_ref.shape[0]

    def prepare_head():
        kmax = jnp.float32(0.0)
        for c in range(seq // tile):
            rows = slice(c * tile, (c + 1) * tile)
            vt_ref[0:dv, rows] = v_ref[rows, :].T
            kmax = jnp.maximum(kmax, _max_half_row_norm_sq(k_ref[rows, :], dk))
        extra = lax.broadcasted_iota(jnp.int32, (ATTN_VT_PAD, seq), 0)
        vt_ref[dv:dv + ATTN_VT_PAD, :] = jnp.where(extra == 0, 1.0, 0.0).astype(BF16)
        kmax_ref[0] = kmax
        krow = lax.broadcasted_iota(jnp.int32, (tile, tile), 0)
        qcol = lax.broadcasted_iota(jnp.int32, (tile, tile), 1)
        rel = (qcol - krow).astype(F32)
        bias_ref[0] = -slope * rel
        allowed = (krow // CHUNK) <= (qcol // CHUNK)
        bias_ref[1] = jnp.where(allowed, -slope * jnp.abs(rel), MASK_VALUE)

    prepare_head()
    lax.fori_loop(0, seq // tile,
                  functools.partial(_attn_query_tile, lam_ref=lam_ref, q_ref=q_ref, k_ref=k_ref,
                                    gain_ref=gain_ref, o_ref=o_ref, vt_ref=vt_ref, acc_ref=acc_ref,
                                    m_ref=m_ref, s_ref=s_ref, p_ref=p_ref, al_ref=al_ref,
                                    kmax_ref=kmax_ref, bias_ref=bias_ref, slope=slope, tile=tile,
                                    lam_init=lam_init),
                  0)


def _attn_query_tile(qi, carry, *, lam_ref, q_ref, k_ref, gain_ref, o_ref, vt_ref, acc_ref, m_ref,
                     s_ref, p_ref, al_ref, kmax_ref, bias_ref, slope, tile, lam_init):
    dk = A_QK_DIM
    dv = A_V_DIM
    q_rows = pl.ds(pl.multiple_of(qi * tile, tile), tile)
    m_ref[...] = jnp.full(m_ref.shape, MASK_VALUE, F32)
    acc_ref[...] = jnp.zeros_like(acc_ref)
    bias_full = bias_ref.at[0]

    def scores(j, slot):
        kt = k_ref[pl.ds(pl.multiple_of(j * tile, tile), tile), :]
        for c in range(2):
            s_ref[slot, c] = lax.dot_general(kt[:, c * dk:(c + 1) * dk], q_ref[q_rows, c * dk:(c + 1) * dk],
                                             (((1,), (1,)), ((), ())), preferred_element_type=F32)

    def softmax_step(slot, bias, shift):
        for c in range(2):
            m_prev = m_ref[c]
            m_new = jnp.maximum(m_prev, jnp.max(s_ref[slot, c] + bias[...], axis=0, keepdims=True) + shift)
            m_ref[c] = m_new
            p_ref[slot, c] = jnp.exp2((bias[...] - (m_new - shift)) + s_ref[slot, c]).astype(BF16)
            al_ref[slot, c] = jnp.exp2(m_prev - m_new)

    def accumulate(j, slot):
        vt = vt_ref[:, pl.ds(pl.multiple_of(j * tile, tile), tile)]
        for c in range(2):
            acc_ref[c] = al_ref[slot, c] * acc_ref[c] + jnp.dot(vt, p_ref[slot, c], preferred_element_type=F32)

    def tile_of(t):
        return jnp.clip(qi - 1 - t, 0, qi)

    def shift_of(t, n_valid):
        return jnp.where(t < n_valid, -slope * jnp.asarray((t + 1) * tile).astype(F32), MASK_VALUE)

    scores(qi, 0)
    scores(tile_of(0), 1)
    softmax_step(0, bias_ref.at[1], 0.0)
    softmax_step(1, bias_full, shift_of(0, qi))

    qk_bound = jnp.sqrt(_max_half_row_norm_sq(q_ref[q_rows, :], dk) * kmax_ref[0]) + ATTN_BOUND_MARGIN
    reach = (qk_bound + UNDERFLOW_EXP2 - jnp.min(m_ref[...])) / (slope * tile)
    n_keep = jnp.minimum(qi, jnp.clip(reach, 0.0, 1e6).astype(jnp.int32) + 1)
    pairs = lax.div(jnp.maximum(n_keep - 1, 0) + 1, 2)

    @pl.when(pairs > 0)
    def _():
        scores(tile_of(1), 0)
        scores(tile_of(2), 1)

    def stage(a, lookahead):
        accumulate(tile_of(a - 2), 0)
        accumulate(tile_of(a - 1), 1)
        softmax_step(0, bias_full, shift_of(a, n_keep))
        if lookahead:
            scores(tile_of(a + 2), 0)
        softmax_step(1, bias_full, shift_of(a + 1, n_keep))
        if lookahead:
            scores(tile_of(a + 3), 1)

    def pair(g, carry):
        stage(1 + 2 * g, True)
        return carry

    lax.fori_loop(0, pairs - 1, pair, 0)

    @pl.when(pairs > 0)
    def _():
        stage(2 * pairs - 1, False)

    last = 2 * pairs + 1
    accumulate(tile_of(last - 2), 0)
    accumulate(tile_of(last - 1), 1)

    lam_vec = lam_ref[...]
    lam = (jnp.exp(jnp.sum(lam_vec[0:1] * lam_vec[1:2], axis=1, keepdims=True))
           - jnp.exp(jnp.sum(lam_vec[2:3] * lam_vec[3:4], axis=1, keepdims=True)) + lam_init)
    o = (acc_ref[0, 0:dv] / acc_ref[0, dv:dv + 1]
         - lam * (acc_ref[1, 0:dv] / acc_ref[1, dv:dv + 1]))
    scale = lax.rsqrt(jnp.mean(o * o, axis=0, keepdims=True) + EPS) * (1.0 - lam_init)
    o_ref[q_rows, :] = ((o * scale).T * gain_ref[...]).astype(o_ref.dtype)
    return carry


def _diff_attention(a, lam_vecs, gain, lam_init, batch, seq_len):
    T = a.shape[0]
    H, Dv = A_HEADS, A_V_DIM
    W = H * Dv
    tile = ATTN_TILE
    assert tile % CHUNK == 0 and seq_len % tile == 0
    slopes = jnp.asarray(ALIBI_SLOPES, F32)
    return pl.pallas_call(
        functools.partial(_attn_kernel, tile=tile, lam_init=lam_init),
        grid=(batch, H),
        in_specs=[pl.BlockSpec(memory_space=pltpu.SMEM),
                  pl.BlockSpec((4, A_QK_DIM), lambda b, h: (0, 0)),
                  pl.BlockSpec((seq_len, Dv), lambda b, h: (b, h)),
                  pl.BlockSpec((seq_len, Dv), lambda b, h: (b, H + h)),
                  pl.BlockSpec((seq_len, Dv), lambda b, h: (b, 2 * H + h)),
                  pl.BlockSpec((1, Dv), lambda b, h: (0, h))],
        out_specs=pl.BlockSpec((seq_len, Dv), lambda b, h: (b, h)),
        out_shape=jax.ShapeDtypeStruct((T, W), BF16),
        scratch_shapes=[pltpu.VMEM((Dv + ATTN_VT_PAD, seq_len), BF16),
                        pltpu.VMEM((2, Dv + ATTN_VT_PAD, tile), F32),
                        pltpu.VMEM((2, 1, tile), F32),
                        pltpu.VMEM((2, 2, tile, tile), F32),
                        pltpu.VMEM((2, 2, tile, tile), BF16),
                        pltpu.VMEM((2, 2, 1, tile), F32),
                        pltpu.SMEM((1,), F32),
                        pltpu.VMEM((2, tile, tile), F32)],
        compiler_params=_params(("parallel", "parallel"),
                                9 * _nbytes((seq_len, Dv), BF16),
                                2 * _nbytes((Dv, tile), F32), 6 * _nbytes((tile, tile), F32),
                                4 * _nbytes((tile, tile), BF16)),
        name="diff_attention",
    )(slopes, lam_vecs, a, a, a, gain.reshape(1, W))


def _merge_kernel(am_ref, aa_ref, wm_ref, wa_ref, gm_ref, ga_ref, o_ref):
    for cols in _col_blocks(o_ref.shape[1]):
        ym = jnp.dot(am_ref[...], wm_ref[:, cols], preferred_element_type=F32)
        ya = jnp.dot(aa_ref[...], wa_ref[:, cols], preferred_element_type=F32)
        o_ref[:, cols] = (gm_ref[:, cols].astype(F32) * ym + ga_ref[:, cols].astype(F32) * ya).astype(o_ref.dtype)


def _branch_merge(hm, ha, wm, wa, g, tm=1024, tn=1024):
    T, K = hm.shape
    N = wm.shape[1]
    nj = N // tn
    return pl.pallas_call(
        _merge_kernel,
        grid=(T // tm, nj),
        in_specs=[pl.BlockSpec((tm, K), lambda i, j: (i, 0)),
                  pl.BlockSpec((tm, K), lambda i, j: (i, 0)),
                  pl.BlockSpec((K, tn), lambda i, j: (0, j)),
                  pl.BlockSpec((K, tn), lambda i, j: (0, j)),
                  pl.BlockSpec((tm, tn), lambda i, j: (i, j)),
                  pl.BlockSpec((tm, tn), lambda i, j: (i, nj + j))],
        out_specs=pl.BlockSpec((tm, tn), lambda i, j: (i, j)),
        out_shape=jax.ShapeDtypeStruct((T, N), BF16),
        compiler_params=_params(("parallel", "parallel"),
                                4 * _nbytes((tm, K), BF16), 4 * _nbytes((K, tn), BF16),
                                6 * _nbytes((tm, tn), BF16), 2 * _nbytes((tm, tn), F32)),
        name="branch_merge",
    )(hm, ha, wm, wa, g, g)


def _outproj_kernel(mix_ref, w_ref, x_ref, gpost_ref, gpre_ref, x1_ref, h2_ref, *, row_split):
    for rows in _row_blocks(x1_ref.shape[0], row_split):
        y = jnp.dot(mix_ref[rows, :], w_ref[...], preferred_element_type=F32)
        x1 = x_ref[rows, :] + y * _rms_scale(y) * gpost_ref[...]
        x1_ref[rows, :] = x1
        h2_ref[rows, :] = (x1 * _rms_scale(x1) * gpre_ref[...]).astype(h2_ref.dtype)


def _outproj(mix, w, x, g_post, g_pre, tm=512, row_split=4):
    T, K = mix.shape
    D = w.shape[1]
    row = lambda i: (i, 0)
    fixed = lambda i: (0, 0)
    return pl.pallas_call(
        functools.partial(_outproj_kernel, row_split=row_split),
        grid=(T // tm,),
        in_specs=[pl.BlockSpec((tm, K), row),
                  pl.BlockSpec((K, D), fixed),
                  pl.BlockSpec((tm, D), row),
                  pl.BlockSpec((1, D), fixed),
                  pl.BlockSpec((1, D), fixed)],
        out_specs=[pl.BlockSpec((tm, D), row), pl.BlockSpec((tm, D), row)],
        out_shape=[jax.ShapeDtypeStruct((T, D), F32), jax.ShapeDtypeStruct((T, D), BF16)],
        compiler_params=_params(("parallel",),
                                2 * _nbytes((tm, K), BF16), 2 * _nbytes((K, D), BF16),
                                4 * _nbytes((tm, D), F32), 2 * _nbytes((tm, D), BF16),
                                2 * _nbytes((tm, D), F32)),
        name="outproj_norm",
    )(mix, w, x, g_post.reshape(1, D), g_pre.reshape(1, D))


def _gelu_tanh(x):
    return 0.5 * x * (1.0 + jnp.tanh(math.sqrt(2.0 / math.pi) * (x + 0.044715 * (x * x * x))))


def _ffn_up_kernel(h_ref, wg_ref, wv_ref, cwg_ref, cwv_ref, cbg_ref, cbv_ref, o_ref,
                   carry_ref, buf_ref, *, tiles_per_seq, taps, row_split):
    i = pl.program_id(0)
    j = pl.program_id(1)
    first = (i % tiles_per_seq) == 0
    _load_conv_halo(buf_ref.at[0], carry_ref.at[j, 0], first)
    _load_conv_halo(buf_ref.at[1], carry_ref.at[j, 1], first)
    for rows in _row_blocks(o_ref.shape[0], row_split):
        for cols in _col_blocks(o_ref.shape[1]):
            h = h_ref[rows, :]
            gate = _causal_conv_rows(jnp.dot(h, wg_ref[:, cols], preferred_element_type=F32), rows, cols,
                                     buf_ref.at[0], carry_ref.at[j, 0], cwg_ref, cbg_ref, taps)
            val = _causal_conv_rows(jnp.dot(h, wv_ref[:, cols], preferred_element_type=F32), rows, cols,
                                    buf_ref.at[1], carry_ref.at[j, 1], cwv_ref, cbv_ref, taps)
            o_ref[rows, cols] = (_gelu_tanh(gate) * val).astype(o_ref.dtype)


def _ffn_up(h2, w_up, cw, cb, seq_len, tm=1024, tf=512, row_split=1):
    T, K = h2.shape
    F = w_up.shape[1] // 2
    taps = cw.shape[0]
    nj = F // tf
    gate_col = lambda i, j: (0, j)
    val_col = lambda i, j: (0, nj + j)
    cb2 = cb.reshape(1, 2 * F)
    return pl.pallas_call(
        functools.partial(_ffn_up_kernel, tiles_per_seq=seq_len // tm, taps=taps, row_split=row_split),
        grid=(T // tm, nj),
        in_specs=[pl.BlockSpec((tm, K), lambda i, j: (i, 0)),
                  pl.BlockSpec((K, tf), gate_col),
                  pl.BlockSpec((K, tf), val_col),
                  pl.BlockSpec((taps, tf), gate_col),
                  pl.BlockSpec((taps, tf), val_col),
                  pl.BlockSpec((1, tf), gate_col),
                  pl.BlockSpec((1, tf), val_col)],
        out_specs=pl.BlockSpec((tm, tf), lambda i, j: (i, j)),
        out_shape=jax.ShapeDtypeStruct((T, F), BF16),
        scratch_shapes=[pltpu.VMEM((nj, 2, V7X_SUBLANES, tf), F32),
                        pltpu.VMEM((2, tm + V7X_SUBLANES, tf), F32)],
        compiler_params=_params(("arbitrary", "arbitrary"),
                                2 * _nbytes((tm, K), BF16), 4 * _nbytes((K, tf), BF16),
                                2 * _nbytes((tm, tf), BF16), 4 * _nbytes((tm, tf), F32)),
        name="ffn_up_conv_gelu",
    )(h2, w_up, w_up, cw, cw, cb2, cb2)


def _ffn_down_kernel(a_ref, w_ref, x1_ref, g_ref, o_ref, *, row_split):
    for rows in _row_blocks(o_ref.shape[0], row_split):
        y = jnp.dot(a_ref[rows, :], w_ref[...], preferred_element_type=F32)
        o_ref[rows, :] = x1_ref[rows, :] + y * _rms_scale(y) * g_ref[...]


def _ffn_down(act, w, x1, g, tm=512, row_split=4):
    T, F = act.shape
    D = w.shape[1]
    return pl.pallas_call(
        functools.partial(_ffn_down_kernel, row_split=row_split),
        grid=(T // tm,),
        in_specs=[pl.BlockSpec((tm, F), lambda i: (i, 0)),
                  pl.BlockSpec((F, D), lambda i: (0, 0), pipeline_mode=pl.Buffered(1)),
                  pl.BlockSpec((tm, D), lambda i: (i, 0)),
                  pl.BlockSpec((1, D), lambda i: (0, 0))],
        out_specs=pl.BlockSpec((tm, D), lambda i: (i, 0)),
        out_shape=jax.ShapeDtypeStruct((T, D), F32),
        compiler_params=_params(("parallel",),
                                2 * _nbytes((tm, F), BF16), _nbytes((F, D), BF16),
                                4 * _nbytes((tm, D), F32)),
        name="ffn_down_norm",
    )(act, w, x1, g.reshape(1, D))


def _layer(x, l, g_pre_mix, w_in, b_gates, m_conv_w, m_conv_b, m_head_norm,
           lambda_q1, lambda_k1, lambda_q2, lambda_k2, a_head_norm,
           w_branch_m, w_branch_a, w_out, g_post_mix, g_pre_ffn,
           w_up, ffn_conv_w, ffn_conv_b, w_down, g_post_ffn, batch, seq_len):
    D = x.shape[1]
    mw = M_HEADS * M_HEAD_DIM
    aw = A_HEADS * A_V_DIM
    n_gate = 2 * M_HEADS
    pre = 4 * mw
    post0 = pre + n_gate
    lam_init = 0.8 - 0.6 * math.exp(-0.3 * l)

    wt = w_in.T
    wt_pre = _cast_rows(wt, 0, 0, pre)
    wt_post = _cast_rows(wt, pre, n_gate, 3 * aw + 2 * D)
    wt_gate = jnp.pad(wt[pre:post0], ((0, V7X_LANES - n_gate), (0, 0))).astype(BF16)
    b_gate = jnp.pad(b_gates.astype(F32), (0, V7X_LANES - n_gate))
    a_scale = jnp.concatenate([jnp.full((aw,), A_QK_DIM ** -0.5 * LOG2E, F32), jnp.ones((2 * aw,), F32)])
    lam_vecs = jnp.stack([lambda_q1, lambda_k1, lambda_q2, lambda_k2]).astype(F32)

    h = _rmsnorm(x, g_pre_mix)
    qk = _proj_conv_silu(h, wt_pre, 2 * mw, m_conv_w, m_conv_b, seq_len)
    v = _proj_act(h, wt_pre, 2 * mw, mw, None, None, "identity", BF16)
    og = _proj_act(h, wt_pre, 3 * mw, mw, None, None, "sigmoid", BF16)
    gates = _proj_act(h, wt_gate, 0, V7X_LANES, jnp.ones((V7X_LANES,), F32), b_gate, "identity", F32)
    a = _proj_act(h, wt_post, 0, 3 * aw, a_scale, jnp.zeros((3 * aw,), F32), "identity", BF16)
    g = _proj_act(h, wt_post, 3 * aw, 2 * D, None, None, "sigmoid", BF16)

    hm = _mlstm(qk, v, og, gates, m_head_norm, batch, seq_len)
    ha = _diff_attention(a, lam_vecs, a_head_norm, lam_init, batch, seq_len)

    mix = _branch_merge(hm, ha, w_branch_m.astype(BF16), w_branch_a.astype(BF16), g)
    x1, h2 = _outproj(mix, w_out.astype(BF16), x, g_post_mix, g_pre_ffn)
    act = _ffn_up(h2, w_up.astype(BF16), ffn_conv_w, ffn_conv_b, seq_len)
    return _ffn_down(act, w_down.astype(BF16), x1, g_post_ffn)


def kernel(x, g_pre_mix, w_in, b_gates, m_conv_w, m_conv_b, m_head_norm, lambda_q1, lambda_k1, lambda_q2, lambda_k2, a_head_norm, w_branch_m, w_branch_a, w_out, g_post_mix, g_pre_ffn, w_up, ffn_conv_w, ffn_conv_b, w_down, g_post_ffn):
    B, S, D = x.shape
    layers = (g_pre_mix, w_in, b_gates, m_conv_w, m_conv_b, m_head_norm, lambda_q1, lambda_k1,
              lambda_q2, lambda_k2, a_head_norm, w_branch_m, w_branch_a, w_out, g_post_mix,
              g_pre_ffn, w_up, ffn_conv_w, ffn_conv_b, w_down, g_post_ffn)
    y = x.reshape(B * S, D)
    for l in range(w_in.shape[0]):
        y = _layer(y, l, *[p[l] for p in layers], batch=B, seq_len=S)
    return y.reshape(B, S, D)
```

```python
import functools
import math

import numpy as np
import jax
import jax.numpy as jnp
from jax import lax
from jax.experimental import pallas as pl
from jax.experimental.pallas import tpu as pltpu

F32 = jnp.float32
BF16 = jnp.bfloat16

EPS = 1e-6
CHUNK = 64
M_HEADS = 8
M_HEAD_DIM = 256
A_HEADS = 8
A_QK_DIM = 128
A_V_DIM = 256
ALIBI_SLOPES = 2.0 ** (-8.0 * np.arange(1, A_HEADS + 1) / A_HEADS)

V7X_LANES = 128
V7X_SUBLANES = 8
V7X_MXU_WIDTH = 256
V7X_NUM_MXU = 2
V7X_SCOPED_VMEM_BYTES = 60000 * 1024
COMPILER_TEMP_BYTES = 16 * 1024 * 1024

MLSTM_CHUNK = V7X_MXU_WIDTH
ATTN_TILE = 512
ATTN_VT_PAD = 16
UNDERFLOW_EXP2 = 136.0
ATTN_BOUND_MARGIN = 1.0
MASK_VALUE = -1e30
LOG2E = math.log2(math.e)


def _vmem_limit(*buffer_bytes):
    return int(min(sum(buffer_bytes) + COMPILER_TEMP_BYTES, V7X_SCOPED_VMEM_BYTES))


def _nbytes(shape, dtype):
    return int(np.prod(shape)) * jnp.dtype(dtype).itemsize


def _params(semantics, *buffer_bytes):
    return pltpu.CompilerParams(dimension_semantics=semantics,
                                vmem_limit_bytes=_vmem_limit(*buffer_bytes))


def _rms_scale(y):
    return lax.rsqrt(jnp.mean(y * y, axis=-1, keepdims=True) + EPS)


def _sigmoid(x):
    return 0.5 * jnp.tanh(0.5 * x) + 0.5


def _col_blocks(width):
    sub = V7X_NUM_MXU * V7X_MXU_WIDTH
    if width % sub:
        return [slice(0, width)]
    return [slice(c, c + sub) for c in range(0, width, sub)]


def _rmsnorm_kernel(x_ref, g_ref, o_ref):
    x = x_ref[...]
    o_ref[...] = (x * _rms_scale(x) * g_ref[...]).astype(o_ref.dtype)


def _rmsnorm(x, g, tm=512):
    T, D = x.shape
    return pl.pallas_call(
        _rmsnorm_kernel,
        grid=(T // tm,),
        in_specs=[pl.BlockSpec((tm, D), lambda i: (i, 0)),
                  pl.BlockSpec((1, D), lambda i: (0, 0))],
        out_specs=pl.BlockSpec((tm, D), lambda i: (i, 0)),
        out_shape=jax.ShapeDtypeStruct((T, D), BF16),
        compiler_params=_params(("parallel",), 2 * _nbytes((tm, D), F32), 2 * _nbytes((tm, D), BF16)),
        name="rmsnorm",
    )(x, g.reshape(1, D))


def _cast_rows_kernel(main_ref, next_ref, o_ref, *, lead):
    if lead:
        rows = jnp.concatenate([main_ref[lead:, :], next_ref[...]], axis=0)
    else:
        rows = main_ref[...]
    o_ref[...] = rows.astype(o_ref.dtype)


def _cast_rows(wt, row0, lead, n_rows, tr=512):
    K = wt.shape[1]
    nxt = max(lead, V7X_SUBLANES)
    assert row0 % tr == 0 and n_rows % tr == 0 and lead % V7X_SUBLANES == 0 and tr % nxt == 0
    i0 = row0 // tr
    return pl.pallas_call(
        functools.partial(_cast_rows_kernel, lead=lead),
        grid=(n_rows // tr,),
        in_specs=[pl.BlockSpec((tr, K), lambda i: (i0 + i, 0)),
                  pl.BlockSpec((nxt, K), lambda i: ((i0 + i + 1) * (tr // nxt), 0))],
        out_specs=pl.BlockSpec((tr, K), lambda i: (i, 0)),
        out_shape=jax.ShapeDtypeStruct((n_rows, K), BF16),
        compiler_params=_params(("parallel",), 2 * _nbytes((tr, K), F32), 2 * _nbytes((tr, K), BF16)),
        name="cast_rows",
    )(wt, wt)


def _load_conv_halo(buf_ref, carry_ref, first_tile):
    halo = V7X_SUBLANES

    @pl.when(first_tile)
    def _():
        buf_ref[0:halo, :] = jnp.zeros((halo, buf_ref.shape[1]), F32)

    @pl.when(jnp.logical_not(first_tile))
    def _():
        buf_ref[0:halo, :] = carry_ref[...]


def _causal_conv_rows(acc, rows, cols, buf_ref, carry_ref, cw_ref, cb_ref, taps):
    halo = V7X_SUBLANES
    r0, r1 = rows.start, rows.stop
    buf_ref[halo + r0:halo + r1, cols] = acc
    if r1 == buf_ref.shape[0] - halo:
        carry_ref[:, cols] = acc[r1 - r0 - halo:, :]
    y = cb_ref[:, cols] + cw_ref[taps - 1:taps, cols] * acc
    window = buf_ref[r0:halo + r1, cols]
    for k in range(taps - 1):
        back = taps - 1 - k
        y = y + cw_ref[k:k + 1, cols] * pltpu.roll(window, back, 0)[halo:]
    return y


def _row_blocks(tm, split):
    step = tm // split
    return [slice(r, r + step) for r in range(0, tm, step)]


def _dot_nt(x, wt):
    return lax.dot_general(x, wt, (((1,), (1,)), ((), ())), preferred_element_type=F32)


def _proj_conv_silu_kernel(h_ref, wt_ref, cw_ref, cb_ref, o_ref, carry_ref, buf_ref, *,
                           tiles_per_seq, taps, row_split):
    i = pl.program_id(0)
    j = pl.program_id(1)
    _load_conv_halo(buf_ref, carry_ref.at[j], (i % tiles_per_seq) == 0)
    for rows in _row_blocks(o_ref.shape[0], row_split):
        for cols in _col_blocks(o_ref.shape[1]):
            acc = _dot_nt(h_ref[rows, :], wt_ref[cols, :])
            y = _causal_conv_rows(acc, rows, cols, buf_ref, carry_ref.at[j], cw_ref, cb_ref, taps)
            o_ref[rows, cols] = (y * _sigmoid(y)).astype(o_ref.dtype)


def _proj_conv_silu(h, wt, n_cols, cw, cb, seq_len, tm=1024, tn=1024, row_split=1):
    T, K = h.shape
    taps = cw.shape[0]
    nj = n_cols // tn
    kern = functools.partial(_proj_conv_silu_kernel, tiles_per_seq=seq_len // tm, taps=taps,
                             row_split=row_split)
    return pl.pallas_call(
        kern,
        grid=(T // tm, nj),
        in_specs=[pl.BlockSpec((tm, K), lambda i, j: (i, 0)),
                  pl.BlockSpec((tn, K), lambda i, j: (j, 0)),
                  pl.BlockSpec((taps, tn), lambda i, j: (0, j)),
                  pl.BlockSpec((1, tn), lambda i, j: (0, j))],
        out_specs=pl.BlockSpec((tm, tn), lambda i, j: (i, j)),
        out_shape=jax.ShapeDtypeStruct((T, n_cols), BF16),
        scratch_shapes=[pltpu.VMEM((nj, V7X_SUBLANES, tn), F32),
                        pltpu.VMEM((tm + V7X_SUBLANES, tn), F32)],
        compiler_params=_params(("arbitrary", "arbitrary"),
                                2 * _nbytes((tm, K), BF16), 2 * _nbytes((K, tn), BF16),
                                2 * _nbytes((tm, tn), BF16), 2 * _nbytes((tm, tn), F32)),
        name="proj_conv_silu",
    )(h, wt, cw, cb.reshape(1, n_cols))


def _proj_act_kernel(h_ref, wt_ref, sc_ref, b_ref, o_ref, *, act, affine):
    for cols in _col_blocks(o_ref.shape[1]):
        acc = _dot_nt(h_ref[...], wt_ref[cols, :])
        if affine:
            acc = acc * sc_ref[:, cols] + b_ref[:, cols]
        if act == "sigmoid":
            acc = _sigmoid(acc)
        o_ref[:, cols] = acc.astype(o_ref.dtype)


def _proj_act(h, wt, row0, n_cols, scale, bias, act, out_dtype, tm=1024, tn=1024):
    T, K = h.shape
    tn = min(tn, n_cols)
    j0 = row0 // tn
    affine = scale is not None
    if not affine:
        scale, bias = jnp.ones((n_cols,), F32), jnp.zeros((n_cols,), F32)
    return pl.pallas_call(
        functools.partial(_proj_act_kernel, act=act, affine=affine),
        grid=(T // tm, n_cols // tn),
        in_specs=[pl.BlockSpec((tm, K), lambda i, j: (i, 0)),
                  pl.BlockSpec((tn, K), lambda i, j: (j0 + j, 0)),
                  pl.BlockSpec((1, tn), lambda i, j: (0, j)),
                  pl.BlockSpec((1, tn), lambda i, j: (0, j))],
        out_specs=pl.BlockSpec((tm, tn), lambda i, j: (i, j)),
        out_shape=jax.ShapeDtypeStruct((T, n_cols), out_dtype),
        compiler_params=_params(("parallel", "parallel"),
                                2 * _nbytes((tm, K), BF16), 2 * _nbytes((K, tn), BF16),
                                2 * _nbytes((tm, tn), out_dtype), _nbytes((tm, tn), F32)),
        name="proj_" + act,
    )(h, wt, scale.reshape(1, n_cols), bias.reshape(1, n_cols))


def _mlstm_kernel(q_ref, k_ref, v_ref, og_ref, g_ref, gain_ref, o_ref, c_ref, n_ref, m_ref, *,
                  heads, head_dim):
    L = q_ref.shape[0]
    H, Dh = heads, head_dim
    q_scale = Dh ** -0.5

    @pl.when(pl.program_id(1) == 0)
    def _():
        c_ref[...] = jnp.zeros_like(c_ref)
        n_ref[...] = jnp.zeros_like(n_ref)
        m_ref[...] = jnp.zeros_like(m_ref)

    gates = g_ref[...]
    logf = jnp.minimum(gates, 0.0) - jnp.log1p(jnp.exp(-jnp.abs(gates)))
    srow = lax.broadcasted_iota(jnp.int32, (L, L), 0)
    tcol = lax.broadcasted_iota(jnp.int32, (L, L), 1)
    causal = srow <= tcol
    cum = jnp.dot((tcol <= srow).astype(F32), logf, precision=lax.Precision.HIGHEST,
                  preferred_element_type=F32)
    u_all = gates - pltpu.roll(cum, V7X_LANES - H, 1)
    u_all_t = u_all.T
    cum_t = cum.T

    for h in range(H):
        hs = slice(h * Dh, (h + 1) * Dh)
        u_row = u_all_t[h:h + 1, :]
        b_row = cum_t[H + h:H + h + 1, :]
        u_keys = jnp.broadcast_to(u_all[:, h:h + 1], (L, L))
        m_prev = m_ref[h, 0:1, 0:1]
        cu = jnp.max(jnp.where(causal, u_keys, -jnp.inf), axis=0, keepdims=True)
        mm = jnp.maximum(m_prev, cu)
        w_intra = jnp.where(causal, jnp.exp(u_keys - mm), 0.0) * q_scale
        w_inter = jnp.exp(m_prev - mm) * q_scale

        qh = q_ref[:, hs]
        kh = k_ref[:, hs]
        vt = v_ref[:, hs].T
        sqk = _dot_nt(kh, qh) * w_intra
        num = (jnp.dot(vt, sqk.astype(BF16), preferred_element_type=F32)
               + w_inter * _dot_nt(c_ref[h].astype(BF16), qh))
        qn = _dot_nt(n_ref[h].astype(BF16), qh)[0:1, :]
        den = jnp.sum(sqk, axis=0, keepdims=True) + w_inter * qn
        hh = num / jnp.maximum(jnp.abs(den), jnp.exp(-(b_row + mm)))
        scale = lax.rsqrt(jnp.mean(hh * hh, axis=0, keepdims=True) + EPS)
        y = (hh * scale).T * gain_ref[:, hs]
        o_ref[:, hs] = (y * og_ref[:, hs].astype(F32)).astype(o_ref.dtype)

        mm_last = mm[:, L - 1:L]
        ws = jnp.exp(u_row - mm_last)
        decay = jnp.exp(m_prev - mm_last)
        c_ref[h] = decay * c_ref[h] + jnp.dot((vt.astype(F32) * ws).astype(BF16), kh,
                                              preferred_element_type=F32)
        ws_rows = jnp.broadcast_to(ws, (V7X_SUBLANES, L)).astype(BF16)
        n_ref[h] = decay * n_ref[h] + jnp.dot(ws_rows, kh, preferred_element_type=F32)
        m_ref[h] = jnp.broadcast_to(b_row[:, L - 1:L] + mm_last, m_ref.shape[1:])


def _mlstm(qk, v, og, gates, gain, batch, seq_len):
    T = qk.shape[0]
    H, Dh, L = M_HEADS, M_HEAD_DIM, MLSTM_CHUNK
    W = H * Dh
    nt = seq_len // L
    row_blk = lambda b, t: (b * nt + t, 0)
    return pl.pallas_call(
        functools.partial(_mlstm_kernel, heads=H, head_dim=Dh),
        grid=(batch, nt),
        in_specs=[pl.BlockSpec((L, W), row_blk),
                  pl.BlockSpec((L, W), lambda b, t: (b * nt + t, 1)),
                  pl.BlockSpec((L, W), row_blk),
                  pl.BlockSpec((L, W), row_blk),
                  pl.BlockSpec((L, V7X_LANES), row_blk),
                  pl.BlockSpec((1, W), lambda b, t: (0, 0))],
        out_specs=pl.BlockSpec((L, W), row_blk),
        out_shape=jax.ShapeDtypeStruct((T, W), BF16),
        scratch_shapes=[pltpu.VMEM((H, Dh, Dh), F32),
                        pltpu.VMEM((H, V7X_SUBLANES, Dh), F32),
                        pltpu.VMEM((H, V7X_SUBLANES, V7X_LANES), F32)],
        compiler_params=_params(("arbitrary", "arbitrary"),
                                10 * _nbytes((L, W), BF16), _nbytes((H, Dh, Dh), F32)),
        name="mlstm",
    )(qk, qk, v, og, gates, gain.reshape(1, W))


def _max_half_row_norm_sq(x, dk):
    sq = x.astype(F32)
    sq = sq * sq
    return jnp.maximum(jnp.max(jnp.sum(sq[:, 0:dk], axis=1, keepdims=True)),
                       jnp.max(jnp.sum(sq[:, dk:2 * dk], axis=1, keepdims=True)))


def _attn_kernel(slopes_ref, lam_ref, q_ref, k_ref, v_ref, gain_ref, o_ref,
                 vt_ref, acc_ref, m_ref, s_ref, p_ref, al_ref, kmax_ref, bias_ref, *, tile, lam_init):
    dk = A_QK_DIM
    dv = A_V_DIM
    slope = slopes_ref[pl.program_id(1)] * LOG2E
    seq = k_ref.shape[0]

    def prepare_head():
        kmax = jnp.float32(0.0)
        for c in range(seq // tile):
            rows = slice(c * tile, (c + 1) * tile)
            vt_ref[0:dv, rows] = v_ref[rows, :].T
            kmax = jnp.maximum(kmax, _max_half_row_norm_sq(k_ref[rows, :], dk))
        extra = lax.broadcasted_iota(jnp.int32, (ATTN_VT_PAD, seq), 0)
        vt_ref[dv:dv + ATTN_VT_PAD, :] = jnp.where(extra == 0, 1.0, 0.0).astype(BF16)
        kmax_ref[0] = kmax
        krow = lax.broadcasted_iota(jnp.int32, (tile, tile), 0)
        qcol = lax.broadcasted_iota(jnp.int32, (tile, tile), 1)
        rel = (qcol - krow).astype(F32)
        bias_ref[0] = -slope * rel
        allowed = (krow // CHUNK) <= (qcol // CHUNK)
        bias_ref[1] = jnp.where(allowed, -slope * jnp.abs(rel), MASK_VALUE)

    prepare_head()
    lax.fori_loop(0, seq // tile,
                  functools.partial(_attn_query_tile, lam_ref=lam_ref, q_ref=q_ref, k_ref=k_ref,
                                    gain_ref=gain_ref, o_ref=o_ref, vt_ref=vt_ref, acc_ref=acc_ref,
                                    m_ref=m_ref, s_ref=s_ref, p_ref=p_ref, al_ref=al_ref,
                                    kmax_ref=kmax_ref, bias_ref=bias_ref, slope=slope, tile=tile,
                                    lam_init=lam_init),
                  0)


def _attn_query_tile(qi, carry, *, lam_ref, q_ref, k_ref, gain_ref, o_ref, vt_ref, acc_ref, m_ref,
                     s_ref, p_ref, al_ref, kmax_ref, bias_ref, slope, tile, lam_init):
    dk = A_QK_DIM
    dv = A_V_DIM
    q_rows = pl.ds(pl.multiple_of(qi * tile, tile), tile)
    m_ref[...] = jnp.full(m_ref.shape, MASK_VALUE, F32)
    acc_ref[...] = jnp.zeros_like(acc_ref)
    bias_full = bias_ref.at[0]

    def scores(j, slot):
        kt = k_ref[pl.ds(pl.multiple_of(j * tile, tile), tile), :]
        for c in range(2):
            s_ref[slot, c] = lax.dot_general(kt[:, c * dk:(c + 1) * dk], q_ref[q_rows, c * dk:(c + 1) * dk],
                                             (((1,), (1,)), ((), ())), preferred_element_type=F32)

    def softmax_step(slot, bias, shift):
        for c in range(2):
            m_prev = m_ref[c]
            m_new = jnp.maximum(m_prev, jnp.max(s_ref[slot, c] + bias[...], axis=0, keepdims=True) + shift)
            m_ref[c] = m_new
            p_ref[slot, c] = jnp.exp2((bias[...] - (m_new - shift)) + s_ref[slot, c]).astype(BF16)
            al_ref[slot, c] = jnp.exp2(m_prev - m_new)

    def accumulate(j, slot):
        vt = vt_ref[:, pl.ds(pl.multiple_of(j * tile, tile), tile)]
        for c in range(2):
            acc_ref[c] = al_ref[slot, c] * acc_ref[c] + jnp.dot(vt, p_ref[slot, c], preferred_element_type=F32)

    def tile_of(t):
        return jnp.clip(qi - 1 - t, 0, qi)

    def shift_of(t, n_valid):
        return jnp.where(t < n_valid, -slope * jnp.asarray((t + 1) * tile).astype(F32), MASK_VALUE)

    scores(qi, 0)
    scores(tile_of(0), 1)
    softmax_step(0, bias_ref.at[1], 0.0)
    softmax_step(1, bias_full, shift_of(0, qi))

    qk_bound = jnp.sqrt(_max_half_row_norm_sq(q_ref[q_rows, :], dk) * kmax_ref[0]) + ATTN_BOUND_MARGIN
    reach = (qk_bound + UNDERFLOW_EXP2 - jnp.min(m_ref[...])) / (slope * tile)
    n_keep = jnp.minimum(qi, jnp.clip(reach, 0.0, 1e6).astype(jnp.int32) + 1)
    pairs = lax.div(jnp.maximum(n_keep - 1, 0) + 1, 2)

    @pl.when(pairs > 0)
    def _():
        scores(tile_of(1), 0)
        scores(tile_of(2), 1)

    def stage(a, lookahead):
        accumulate(tile_of(a - 2), 0)
        accumulate(tile_of(a - 1), 1)
        softmax_step(0, bias_full, shift_of(a, n_keep))
        if lookahead:
            scores(tile_of(a + 2), 0)
        softmax_step(1, bias_full, shift_of(a + 1, n_keep))
        if lookahead:
            scores(tile_of(a + 3), 1)

    def pair(g, carry):
        stage(1 + 2 * g, True)
        return carry

    lax.fori_loop(0, pairs - 1, pair, 0)

    @pl.when(pairs > 0)
    def _():
        stage(2 * pairs - 1, False)

    last = 2 * pairs + 1
    accumulate(tile_of(last - 2), 0)
    accumulate(tile_of(last - 1), 1)

    lam_vec = lam_ref[...]
    lam = (jnp.exp(jnp.sum(lam_vec[0:1] * lam_vec[1:2], axis=1, keepdims=True))
           - jnp.exp(jnp.sum(lam_vec[2:3] * lam_vec[3:4], axis=1, keepdims=True)) + lam_init)
    o = (acc_ref[0, 0:dv] / acc_ref[0, dv:dv + 1]
         - lam * (acc_ref[1, 0:dv] / acc_ref[1, dv:dv + 1]))
    scale = lax.rsqrt(jnp.mean(o * o, axis=0, keepdims=True) + EPS) * (1.0 - lam_init)
    o_ref[q_rows, :] = ((o * scale).T * gain_ref[...]).astype(o_ref.dtype)
    return carry


def _diff_attention(a, lam_vecs, gain, lam_init, batch, seq_len):
    T = a.shape[0]
    H, Dv = A_HEADS, A_V_DIM
    W = H * Dv
    tile = ATTN_TILE
    assert tile % CHUNK == 0 and seq_len % tile == 0
    slopes = jnp.asarray(ALIBI_SLOPES, F32)
    return pl.pallas_call(
        functools.partial(_attn_kernel, tile=tile, lam_init=lam_init),
        grid=(batch, H),
        in_specs=[pl.BlockSpec(memory_space=pltpu.SMEM),
                  pl.BlockSpec((4, A_QK_DIM), lambda b, h: (0, 0)),
                  pl.BlockSpec((seq_len, Dv), lambda b, h: (b, h)),
                  pl.BlockSpec((seq_len, Dv), lambda b, h: (b, H + h)),
                  pl.BlockSpec((seq_len, Dv), lambda b, h: (b, 2 * H + h)),
                  pl.BlockSpec((1, Dv), lambda b, h: (0, h))],
        out_specs=pl.BlockSpec((seq_len, Dv), lambda b, h: (b, h)),
        out_shape=jax.ShapeDtypeStruct((T, W), BF16),
        scratch_shapes=[pltpu.VMEM((Dv + ATTN_VT_PAD, seq_len), BF16),
                        pltpu.VMEM((2, Dv + ATTN_VT_PAD, tile), F32),
                        pltpu.VMEM((2, 1, tile), F32),
                        pltpu.VMEM((2, 2, tile, tile), F32),
                        pltpu.VMEM((2, 2, tile, tile), BF16),
                        pltpu.VMEM((2, 2, 1, tile), F32),
                        pltpu.SMEM((1,), F32),
                        pltpu.VMEM((2, tile, tile), F32)],
        compiler_params=_params(("parallel", "parallel"),
                                9 * _nbytes((seq_len, Dv), BF16),
                                2 * _nbytes((Dv, tile), F32), 6 * _nbytes((tile, tile), F32),
                                4 * _nbytes((tile, tile), BF16)),
        name="diff_attention",
    )(slopes, lam_vecs, a, a, a, gain.reshape(1, W))


def _merge_kernel(am_ref, aa_ref, wm_ref, wa_ref, gm_ref, ga_ref, o_ref):
    for cols in _col_blocks(o_ref.shape[1]):
        ym = jnp.dot(am_ref[...], wm_ref[:, cols], preferred_element_type=F32)
        ya = jnp.dot(aa_ref[...], wa_ref[:, cols], preferred_element_type=F32)
        o_ref[:, cols] = (gm_ref[:, cols].astype(F32) * ym + ga_ref[:, cols].astype(F32) * ya).astype(o_ref.dtype)


def _branch_merge(hm, ha, wm, wa, g, tm=1024, tn=1024):
    T, K = hm.shape
    N = wm.shape[1]
    nj = N // tn
    return pl.pallas_call(
        _merge_kernel,
        grid=(T // tm, nj),
        in_specs=[pl.BlockSpec((tm, K), lambda i, j: (i, 0)),
                  pl.BlockSpec((tm, K), lambda i, j: (i, 0)),
                  pl.BlockSpec((K, tn), lambda i, j: (0, j)),
                  pl.BlockSpec((K, tn), lambda i, j: (0, j)),
                  pl.BlockSpec((tm, tn), lambda i, j: (i, j)),
                  pl.BlockSpec((tm, tn), lambda i, j: (i, nj + j))],
        out_specs=pl.BlockSpec((tm, tn), lambda i, j: (i, j)),
        out_shape=jax.ShapeDtypeStruct((T, N), BF16),
        compiler_params=_params(("parallel", "parallel"),
                                4 * _nbytes((tm, K), BF16), 4 * _nbytes((K, tn), BF16),
                                6 * _nbytes((tm, tn), BF16), 2 * _nbytes((tm, tn), F32)),
        name="branch_merge",
    )(hm, ha, wm, wa, g, g)


def _outproj_kernel(mix_ref, w_ref, x_ref, gpost_ref, gpre_ref, x1_ref, h2_ref, *, row_split):
    for rows in _row_blocks(x1_ref.shape[0], row_split):
        y = jnp.dot(mix_ref[rows, :], w_ref[...], preferred_element_type=F32)
        x1 = x_ref[rows, :] + y * _rms_scale(y) * gpost_ref[...]
        x1_ref[rows, :] = x1
        h2_ref[rows, :] = (x1 * _rms_scale(x1) * gpre_ref[...]).astype(h2_ref.dtype)


def _outproj(mix, w, x, g_post, g_pre, tm=512, row_split=4):
    T, K = mix.shape
    D = w.shape[1]
    row = lambda i: (i, 0)
    fixed = lambda i: (0, 0)
    return pl.pallas_call(
        functools.partial(_outproj_kernel, row_split=row_split),
        grid=(T // tm,),
        in_specs=[pl.BlockSpec((tm, K), row),
                  pl.BlockSpec((K, D), fixed),
                  pl.BlockSpec((tm, D), row),
                  pl.BlockSpec((1, D), fixed),
                  pl.BlockSpec((1, D), fixed)],
        out_specs=[pl.BlockSpec((tm, D), row), pl.BlockSpec((tm, D), row)],
        out_shape=[jax.ShapeDtypeStruct((T, D), F32), jax.ShapeDtypeStruct((T, D), BF16)],
        compiler_params=_params(("parallel",),
                                2 * _nbytes((tm, K), BF16), 2 * _nbytes((K, D), BF16),
                                4 * _nbytes((tm, D), F32), 2 * _nbytes((tm, D), BF16),
                                2 * _nbytes((tm, D), F32)),
        name="outproj_norm",
    )(mix, w, x, g_post.reshape(1, D), g_pre.reshape(1, D))


def _gelu_tanh(x):
    return 0.5 * x * (1.0 + jnp.tanh(math.sqrt(2.0 / math.pi) * (x + 0.044715 * (x * x * x))))


def _ffn_up_kernel(h_ref, wg_ref, wv_ref, cwg_ref, cwv_ref, cbg_ref, cbv_ref, o_ref,
                   carry_ref, buf_ref, wb_ref, *, tiles_per_seq, taps, row_split):
    i = pl.program_id(1)

    @pl.when(i == 0)
    def _():
        wb_ref[0] = wg_ref[...].astype(BF16)
        wb_ref[1] = wv_ref[...].astype(BF16)

    first = (i % tiles_per_seq) == 0
    _load_conv_halo(buf_ref.at[0], carry_ref.at[0], first)
    _load_conv_halo(buf_ref.at[1], carry_ref.at[1], first)
    for rows in _row_blocks(o_ref.shape[0], row_split):
        for cols in _col_blocks(o_ref.shape[1]):
            h = h_ref[rows, :]
            gate = _causal_conv_rows(jnp.dot(h, wb_ref[0, :, cols], preferred_element_type=F32), rows, cols,
                                     buf_ref.at[0], carry_ref.at[0], cwg_ref, cbg_ref, taps)
            val = _causal_conv_rows(jnp.dot(h, wb_ref[1, :, cols], preferred_element_type=F32), rows, cols,
                                    buf_ref.at[1], carry_ref.at[1], cwv_ref, cbv_ref, taps)
            o_ref[rows, cols] = (_gelu_tanh(gate) * val).astype(o_ref.dtype)


def _ffn_up(h2, w_up, cw, cb, seq_len, tm=1024, tf=512, row_split=1):
    T, K = h2.shape
    F = w_up.shape[1] // 2
    taps = cw.shape[0]
    nj = F // tf
    gate_col = lambda j, i: (0, j)
    val_col = lambda j, i: (0, nj + j)
    cb2 = cb.reshape(1, 2 * F)
    return pl.pallas_call(
        functools.partial(_ffn_up_kernel, tiles_per_seq=seq_len // tm, taps=taps, row_split=row_split),
        grid=(nj, T // tm),
        in_specs=[pl.BlockSpec((tm, K), lambda j, i: (i, 0)),
                  pl.BlockSpec((K, tf), gate_col),
                  pl.BlockSpec((K, tf), val_col),
                  pl.BlockSpec((taps, tf), gate_col),
                  pl.BlockSpec((taps, tf), val_col),
                  pl.BlockSpec((1, tf), gate_col),
                  pl.BlockSpec((1, tf), val_col)],
        out_specs=pl.BlockSpec((tm, tf), lambda j, i: (i, j)),
        out_shape=jax.ShapeDtypeStruct((T, F), BF16),
        scratch_shapes=[pltpu.VMEM((2, V7X_SUBLANES, tf), F32),
                        pltpu.VMEM((2, tm + V7X_SUBLANES, tf), F32),
                        pltpu.VMEM((2, K, tf), BF16)],
        compiler_params=_params(("arbitrary", "arbitrary"),
                                2 * _nbytes((tm, K), BF16), 4 * _nbytes((K, tf), F32),
                                2 * _nbytes((K, tf), BF16),
                                2 * _nbytes((tm, tf), BF16), 4 * _nbytes((tm, tf), F32)),
        name="ffn_up_conv_gelu",
    )(h2, w_up, w_up, cw, cw, cb2, cb2)


def _ffn_down_kernel(a_ref, w_ref, x1_ref, g_ref, o_ref, *, row_split):
    for rows in _row_blocks(o_ref.shape[0], row_split):
        y = jnp.dot(a_ref[rows, :], w_ref[...], preferred_element_type=F32)
        o_ref[rows, :] = x1_ref[rows, :] + y * _rms_scale(y) * g_ref[...]


def _ffn_down(act, w, x1, g, tm=512, row_split=4):
    T, F = act.shape
    D = w.shape[1]
    return pl.pallas_call(
        functools.partial(_ffn_down_kernel, row_split=row_split),
        grid=(T // tm,),
        in_specs=[pl.BlockSpec((tm, F), lambda i: (i, 0)),
                  pl.BlockSpec((F, D), lambda i: (0, 0), pipeline_mode=pl.Buffered(1)),
                  pl.BlockSpec((tm, D), lambda i: (i, 0)),
                  pl.BlockSpec((1, D), lambda i: (0, 0))],
        out_specs=pl.BlockSpec((tm, D), lambda i: (i, 0)),
        out_shape=jax.ShapeDtypeStruct((T, D), F32),
        compiler_params=_params(("parallel",),
                                2 * _nbytes((tm, F), BF16), _nbytes((F, D), BF16),
                                4 * _nbytes((tm, D), F32)),
        name="ffn_down_norm",
    )(act, w, x1, g.reshape(1, D))


def _layer(x, l, g_pre_mix, w_in, b_gates, m_conv_w, m_conv_b, m_head_norm,
           lambda_q1, lambda_k1, lambda_q2, lambda_k2, a_head_norm,
           w_branch_m, w_branch_a, w_out, g_post_mix, g_pre_ffn,
           w_up, ffn_conv_w, ffn_conv_b, w_down, g_post_ffn, batch, seq_len):
    D = x.shape[1]
    mw = M_HEADS * M_HEAD_DIM
    aw = A_HEADS * A_V_DIM
    n_gate = 2 * M_HEADS
    pre = 4 * mw
    post0 = pre + n_gate
    lam_init = 0.8 - 0.6 * math.exp(-0.3 * l)

    wt = w_in.T
    wt_pre = _cast_rows(wt, 0, 0, pre)
    wt_post = _cast_rows(wt, pre, n_gate, 3 * aw + 2 * D)
    wt_gate = jnp.pad(wt[pre:post0], ((0, V7X_LANES - n_gate), (0, 0))).astype(BF16)
    b_gate = jnp.pad(b_gates.astype(F32), (0, V7X_LANES - n_gate))
    a_scale = jnp.concatenate([jnp.full((aw,), A_QK_DIM ** -0.5 * LOG2E, F32), jnp.ones((2 * aw,), F32)])
    lam_vecs = jnp.stack([lambda_q1, lambda_k1, lambda_q2, lambda_k2]).astype(F32)

    h = _rmsnorm(x, g_pre_mix)
    qk = _proj_conv_silu(h, wt_pre, 2 * mw, m_conv_w, m_conv_b, seq_len)
    v = _proj_act(h, wt_pre, 2 * mw, mw, None, None, "identity", BF16)
    og = _proj_act(h, wt_pre, 3 * mw, mw, None, None, "sigmoid", BF16)
    gates = _proj_act(h, wt_gate, 0, V7X_LANES, jnp.ones((V7X_LANES,), F32), b_gate, "identity", F32)
    a = _proj_act(h, wt_post, 0, 3 * aw, a_scale, jnp.zeros((3 * aw,), F32), "identity", BF16)
    g = _proj_act(h, wt_post, 3 * aw, 2 * D, None, None, "sigmoid", BF16)

    hm = _mlstm(qk, v, og, gates, m_head_norm, batch, seq_len)
    ha = _diff_attention(a, lam_vecs, a_head_norm, lam_init, batch, seq_len)

    mix = _branch_merge(hm, ha, w_branch_m.astype(BF16), w_branch_a.astype(BF16), g)
    x1, h2 = _outproj(mix, w_out.astype(BF16), x, g_post_mix, g_pre_ffn)
    act = _ffn_up(h2, w_up, ffn_conv_w, ffn_conv_b, seq_len)
    return _ffn_down(act, w_down.astype(BF16), x1, g_post_ffn)


def kernel(x, g_pre_mix, w_in, b_gates, m_conv_w, m_conv_b, m_head_norm, lambda_q1, lambda_k1, lambda_q2, lambda_k2, a_head_norm, w_branch_m, w_branch_a, w_out, g_post_mix, g_pre_ffn, w_up, ffn_conv_w, ffn_conv_b, w_down, g_post_ffn):
    B, S, D = x.shape
    layers = (g_pre_mix, w_in, b_gates, m_conv_w, m_conv_b, m_head_norm, lambda_q1, lambda_k1,
              lambda_q2, lambda_k2, a_head_norm, w_branch_m, w_branch_a, w_out, g_post_mix,
              g_pre_ffn, w_up, ffn_conv_w, ffn_conv_b, w_down, g_post_ffn)
    y = x.reshape(B * S, D)
    for l in range(w_in.shape[0]):
        y = _layer(y, l, *[p[l] for p in layers], batch=B, seq_len=S)
    return y.reshape(B, S, D)
```
